```python
import jax, jax.numpy as jnp
from jax import lax
import numpy as np

D_MODEL = 2048
BATCH = 2
SEQ = 16384
DEPTH = 2
DEC_BATCH = 32
DEC_SEQ = 64
PAST_LEN = 4096

CHUNK = 64
Q_BLOCK = 128
D_MIX = D_MODEL
FOX_HEADS = 6
FOX_HD = 128
MLA_HEADS = 6
MLA_NOPE = 128
MLA_ROPE = 64
MLA_V = 128
MLA_KV_RANK = 512
GLA_HEADS = 4
GLA_DK = 64
GLA_DV = 128
GLA_GATE_RANK = 16
GLA_GATE_NORM = 16.0

FOX_W = FOX_HEADS * FOX_HD
MLA_W = MLA_HEADS * MLA_V
GLA_W = GLA_HEADS * GLA_DV

ROPE_BASE = 10000.0
LN_EPS = 1e-5
RMS_EPS = 1e-6
ALPHA = (2 * DEPTH) ** 0.25
BETA = (8 * DEPTH) ** -0.25
FOX_SCALE = FOX_HD ** -0.5
MLA_SCALE = (MLA_NOPE + MLA_ROPE) ** -0.5
GLA_SCALE = GLA_DK ** -0.5

IN_SIZES = (FOX_W, FOX_W, FOX_W, FOX_HEADS, FOX_W,
            MLA_HEADS * (MLA_NOPE + MLA_ROPE), MLA_KV_RANK, MLA_ROPE, MLA_W,
            GLA_HEADS * GLA_DK, GLA_HEADS * GLA_DK, GLA_W, GLA_GATE_RANK, GLA_W)
IN_COLS = sum(IN_SIZES)
VALUE_SEGMENTS = (2, 11)

kernel_name = 'hymba_fox_mla_gla_deepnorm_stream_step'


def _layernorm(x, g, b):
    xf = x.astype(jnp.float32)
    mu = jnp.mean(xf, -1, keepdims=True)
    var = jnp.mean(jnp.square(xf - mu), -1, keepdims=True)
    return ((xf - mu) * lax.rsqrt(var + LN_EPS) * g + b).astype(x.dtype)


def _rmsnorm(x, g):
    xf = x.astype(jnp.float32)
    return (xf * lax.rsqrt(jnp.mean(jnp.square(xf), -1, keepdims=True) + RMS_EPS) * g).astype(x.dtype)


def _rope(x, pos):
    half = x.shape[-1] // 2
    inv = ROPE_BASE ** (-jnp.arange(half, dtype=jnp.float32) / half)
    ang = pos.astype(jnp.float32)[:, None] * inv[None, :]
    cos = jnp.cos(ang)[:, None, :]
    sin = jnp.sin(ang)[:, None, :]
    xf = x.astype(jnp.float32)
    x1, x2 = xf[..., :half], xf[..., half:]
    return jnp.concatenate([x1 * cos - x2 * sin, x1 * sin + x2 * cos], -1).astype(x.dtype)


def _split_in(h):
    offs = np.cumsum(np.array(IN_SIZES))[:-1].tolist()
    return jnp.split(h, offs, axis=-1)


def _branch_inputs(x, pos, w_in, b_f, kv_norm, w_gk_up, b_gk):
    B, T, _ = x.shape
    (fq, fk, fv, ff, fg, mq, mc, mr, mg, gq, gk, gv, ggk, gg) = _split_in(x @ w_in)
    fox_logf = jax.nn.log_sigmoid((ff + b_f).astype(jnp.float32))
    mq = mq.reshape(B, T, MLA_HEADS, MLA_NOPE + MLA_ROPE)
    mla_qn = mq[..., :MLA_NOPE]
    mla_qr = _rope(mq[..., MLA_NOPE:], pos)
    mla_c = _rmsnorm(mc, kv_norm)
    mla_kr = _rope(mr[:, :, None, :], pos)[:, :, 0, :]
    gla_g = jax.nn.log_sigmoid((ggk @ w_gk_up + b_gk).astype(jnp.float32)) / GLA_GATE_NORM
    return (fq.reshape(B, T, FOX_HEADS, FOX_HD), fk.reshape(B, T, FOX_HEADS, FOX_HD),
            fv.reshape(B, T, FOX_HEADS, FOX_HD), fox_logf, fg,
            mla_qn, mla_qr, mla_c, mla_kr, mg,
            gq.reshape(B, T, GLA_HEADS, GLA_DK), gk.reshape(B, T, GLA_HEADS, GLA_DK),
            gv.reshape(B, T, GLA_HEADS, GLA_DV), gla_g.reshape(B, T, GLA_HEADS, GLA_DK), gg)


def _attend(q, k, v, bias, mask, scale):
    s = jnp.einsum('bqhd,bkhd->bhqk', q, k, preferred_element_type=jnp.float32) * scale
    if bias is not None:
        s = s + bias
    p = jax.nn.softmax(jnp.where(mask, s, -jnp.inf), axis=-1)
    return jnp.einsum('bhqk,bkhd->bqhd', p.astype(v.dtype), v)


def _block_sweep(q, k, v, scale, chunk_mask, q_cum=None, k_cum=None):
    B, T, H, D = q.shape
    nb = T // Q_BLOCK
    kpos = jnp.arange(k.shape[1])
    xs = (jnp.arange(nb), q.reshape(B, nb, Q_BLOCK, H, D).swapaxes(0, 1))
    if q_cum is not None:
        xs = xs + (q_cum.reshape(B, nb, Q_BLOCK, H).swapaxes(0, 1),)
        kc = jnp.swapaxes(k_cum, 1, 2)[:, :, None, :]

    def blk(args):
        i, q_i = args[0], args[1]
        qpos = i * Q_BLOCK + jnp.arange(Q_BLOCK)
        if chunk_mask:
            mask = (kpos // CHUNK)[None, :] <= (qpos // CHUNK)[:, None]
        else:
            mask = kpos[None, :] <= qpos[:, None]
        bias = None
        if q_cum is not None:
            bias = jnp.swapaxes(args[2], 1, 2)[..., None] - kc
        return _attend(q_i, k, v, bias, mask, scale)

    o = lax.map(blk, xs)
    return o.swapaxes(0, 1).reshape(B, T, H, v.shape[-1])


def _fox_prompt(q, k, v, logf):
    c = jnp.cumsum(logf, axis=1)
    return _block_sweep(q, k, v, FOX_SCALE, False, c, c)


def _fox_sample(q, k, v, logf, ck, cv, clogf):
    P, S = ck.shape[1], q.shape[1]
    c_past = jnp.cumsum(clogf.astype(jnp.float32), axis=1)
    c_past = c_past - c_past[:, -1:]
    c_new = jnp.cumsum(logf, axis=1)
    k_all = jnp.concatenate([ck.astype(k.dtype), k], 1)
    v_all = jnp.concatenate([cv.astype(v.dtype), v], 1)
    c_all = jnp.concatenate([c_past, c_new], 1)
    kpos = jnp.arange(P + S)
    qpos = P + jnp.arange(S)
    mask = kpos[None, :] <= qpos[:, None]
    bias = jnp.swapaxes(c_new, 1, 2)[..., None] - jnp.swapaxes(c_all, 1, 2)[:, :, None, :]
    return _attend(q, k_all, v_all, bias, mask, FOX_SCALE)


def _mla_prompt(qn, qr, c, kr, w_up):
    kv = jnp.einsum('btc,chd->bthd', c, w_up)
    k = jnp.concatenate([kv[..., :MLA_NOPE],
                         jnp.broadcast_to(kr[:, :, None, :], kv.shape[:3] + (MLA_ROPE,))], -1)
    q = jnp.concatenate([qn, qr], -1)
    return _block_sweep(q, k, kv[..., MLA_NOPE:], MLA_SCALE, True)


def _mla_sample(qn, qr, c, kr, c_cache, kr_cache, w_up):
    w_uk, w_uv = w_up[..., :MLA_NOPE], w_up[..., MLA_NOPE:]
    c_all = jnp.concatenate([c_cache.astype(c.dtype), c], 1)
    r_all = jnp.concatenate([kr_cache.astype(kr.dtype), kr], 1)
    q_lat = jnp.einsum('bshn,chn->bshc', qn, w_uk)
    s = (jnp.einsum('bshc,bkc->bhsk', q_lat, c_all, preferred_element_type=jnp.float32)
         + jnp.einsum('bshr,bkr->bhsk', qr, r_all, preferred_element_type=jnp.float32)) * MLA_SCALE
    p = jax.nn.softmax(s, axis=-1)
    o_lat = jnp.einsum('bhsk,bkc->bshc', p.astype(c_all.dtype), c_all)
    return jnp.einsum('bshc,chv->bshv', o_lat, w_uv)


def _gla_chunk(S, xs):
    q, k, v, g = (a.astype(jnp.float32) for a in xs)
    C = q.shape[2]
    b = jnp.cumsum(g, axis=2)
    b_last = b[:, :, -1:, :]
    q_e = q * jnp.exp(b) * GLA_SCALE
    k_e = k * jnp.exp(-b)
    k_t = k * jnp.exp(b_last - b)
    causal = jnp.tril(jnp.ones((C, C), bool))
    A = jnp.where(causal, jnp.einsum('bhid,bhjd->bhij', q_e, k_e), 0.0)
    o = jnp.einsum('bhid,bhde->bhie', q_e, S) + jnp.einsum('bhij,bhje->bhie', A, v)
    S_new = jnp.exp(b_last[:, :, 0, :])[..., None] * S + jnp.einsum('bhjd,bhje->bhde', k_t, v)
    return S_new, o


def _gla_prompt(q, k, v, g):
    B, T, H, _ = q.shape
    n = T // CHUNK

    def to_chunks(a):
        return a.reshape(B, n, CHUNK, H, a.shape[-1]).transpose(1, 0, 3, 2, 4)

    S0 = jnp.zeros((B, H, GLA_DK, GLA_DV), jnp.float32)
    S_fin, o = lax.scan(_gla_chunk, S0, (to_chunks(q), to_chunks(k), to_chunks(v), to_chunks(g)))
    return o.transpose(1, 0, 3, 2, 4).reshape(B, T, H, GLA_DV), S_fin


def _gla_sample(q, k, v, g, S0):
    S_new, o = _gla_chunk(S0.astype(jnp.float32),
                          (q.swapaxes(1, 2), k.swapaxes(1, 2), v.swapaxes(1, 2), g.swapaxes(1, 2)))
    return o.swapaxes(1, 2), S_new


def _merge(x, o_fox, fg, o_mla, mg, o_gla, gg, gla_g, w_out, ln_g, ln_b):
    B, T, _ = x.shape
    o_gla = _rmsnorm(o_gla, gla_g).astype(x.dtype)
    y = jnp.concatenate([o_fox.reshape(B, T, FOX_W) * jax.nn.silu(fg),
                         o_mla.reshape(B, T, MLA_W).astype(x.dtype) * jax.nn.silu(mg),
                         o_gla.reshape(B, T, GLA_W) * jax.nn.silu(gg)], -1)
    return _layernorm(ALPHA * x + y @ w_out, ln_g, ln_b)


def setup_inputs(seed: int = 0) -> dict:
    key = jax.random.key(seed)
    ks = jax.random.split(key, 20)
    nrm = lambda k, s: jax.random.normal(k, s, jnp.float32)
    col_scale = np.concatenate([np.full((s,), BETA if i in VALUE_SEGMENTS else 1.0, np.float32)
                                for i, s in enumerate(IN_SIZES)])
    kv_scale = np.concatenate([np.ones((MLA_NOPE,), np.float32), np.full((MLA_V,), BETA, np.float32)])
    return {
        'x_prompt': nrm(ks[0], (BATCH, SEQ, D_MODEL)),
        'x_sample': nrm(ks[1], (DEC_BATCH, DEC_SEQ, D_MODEL)),
        'cache_fox_k': nrm(ks[2], (DEPTH, DEC_BATCH, PAST_LEN, FOX_HEADS, FOX_HD)),
        'cache_fox_v': nrm(ks[3], (DEPTH, DEC_BATCH, PAST_LEN, FOX_HEADS, FOX_HD)) * BETA,
        'cache_fox_logf': jax.nn.log_sigmoid(4.0 + nrm(ks[4], (DEPTH, DEC_BATCH, PAST_LEN, FOX_HEADS))),
        'cache_mla_ckv': nrm(ks[5], (DEPTH, DEC_BATCH, PAST_LEN, MLA_KV_RANK)),
        'cache_mla_krope': nrm(ks[6], (DEPTH, DEC_BATCH, PAST_LEN, MLA_ROPE)),
        'state_gla': nrm(ks[7], (DEPTH, DEC_BATCH, GLA_HEADS, GLA_DK, GLA_DV)) * 2.0,
        'w_in': nrm(ks[8], (DEPTH, D_MODEL, IN_COLS)) * (D_MODEL ** -0.5) * jnp.asarray(col_scale),
        'b_fox_f': 4.0 + 0.1 * nrm(ks[9], (DEPTH, FOX_HEADS)),
        'mla_kv_norm': 1.0 + 0.02 * nrm(ks[10], (DEPTH, MLA_KV_RANK)),
        'w_mla_kv_up': nrm(ks[11], (DEPTH, MLA_KV_RANK, MLA_HEADS, MLA_NOPE + MLA_V)) * (MLA_KV_RANK ** -0.5) * jnp.asarray(kv_scale),
        'w_gla_gk_up': nrm(ks[12], (DEPTH, GLA_GATE_RANK, GLA_HEADS * GLA_DK)) * (GLA_GATE_RANK ** -0.5),
        'b_gla_gk': 0.1 * nrm(ks[13], (DEPTH, GLA_HEADS * GLA_DK)),
        'gla_norm': 1.0 + 0.02 * nrm(ks[14], (DEPTH, GLA_DV)),
        'w_out': nrm(ks[15], (DEPTH, D_MIX, D_MODEL)) * (D_MIX ** -0.5) * BETA,
        'ln_g': 1.0 + 0.02 * nrm(ks[16], (DEPTH, D_MODEL)),
        'ln_b': 0.02 * nrm(ks[17], (DEPTH, D_MODEL)),
    }


def reference(x_prompt, x_sample, cache_fox_k, cache_fox_v, cache_fox_logf, cache_mla_ckv,
              cache_mla_krope, state_gla, w_in, b_fox_f, mla_kv_norm, w_mla_kv_up,
              w_gla_gk_up, b_gla_gk, gla_norm, w_out, ln_g, ln_b):
    P = cache_fox_k.shape[2]
    pos_p = jnp.arange(x_prompt.shape[1])
    pos_s = P + jnp.arange(x_sample.shape[1])
    xp, xs = x_prompt, x_sample
    p_fk, p_fv, p_fl, p_mc, p_mr, p_gs = [], [], [], [], [], []
    s_fk, s_fv, s_fl, s_mc, s_mr, s_gs = [], [], [], [], [], []
    for l in range(DEPTH):
        (fq, fk, fv, fl, fg, mqn, mqr, mc, mr, mg, gq, gk, gv, gg, ggate) = _branch_inputs(
            xp, pos_p, w_in[l], b_fox_f[l], mla_kv_norm[l], w_gla_gk_up[l], b_gla_gk[l])
        o_f = _fox_prompt(fq, fk, fv, fl)
        o_m = _mla_prompt(mqn, mqr, mc, mr, w_mla_kv_up[l])
        o_g, S_p = _gla_prompt(gq, gk, gv, gg)
        p_fk.append(fk); p_fv.append(fv); p_fl.append(fl)
        p_mc.append(mc); p_mr.append(mr); p_gs.append(S_p)
        xp = _merge(xp, o_f, fg, o_m, mg, o_g, ggate, gla_norm[l], w_out[l], ln_g[l], ln_b[l])

        (fq, fk, fv, fl, fg, mqn, mqr, mc, mr, mg, gq, gk, gv, gg, ggate) = _branch_inputs(
            xs, pos_s, w_in[l], b_fox_f[l], mla_kv_norm[l], w_gla_gk_up[l], b_gla_gk[l])
        o_f = _fox_sample(fq, fk, fv, fl, cache_fox_k[l], cache_fox_v[l], cache_fox_logf[l])
        o_m = _mla_sample(mqn, mqr, mc, mr, cache_mla_ckv[l], cache_mla_krope[l], w_mla_kv_up[l])
        o_g, S_s = _gla_sample(gq, gk, gv, gg, state_gla[l])
        s_fk.append(fk); s_fv.append(fv); s_fl.append(fl)
        s_mc.append(mc); s_mr.append(mr); s_gs.append(S_s)
        xs = _merge(xs, o_f, fg, o_m, mg, o_g, ggate, gla_norm[l], w_out[l], ln_g[l], ln_b[l])

    return (xp, xs,
            jnp.stack(p_fk), jnp.stack(p_fv), jnp.stack(p_fl), jnp.stack(p_mc), jnp.stack(p_mr), jnp.stack(p_gs),
            jnp.stack(s_fk), jnp.stack(s_fv), jnp.stack(s_fl), jnp.stack(s_mc), jnp.stack(s_mr), jnp.stack(s_gs))
```

```python
import functools
import math

import numpy as np
import jax
import jax.numpy as jnp
from jax import lax
from jax.experimental import pallas as pl
from jax.experimental.pallas import tpu as pltpu

F32 = jnp.float32
BF16 = jnp.bfloat16

LANES = 128
SUBLANES = 8
VMEM_LIMIT_BYTES = 60 * 1024 * 1024

CHUNK = 64
FOX_HEADS, FOX_HD = 6, 128
MLA_HEADS, MLA_NOPE, MLA_ROPE, MLA_V, MLA_KV_RANK = 6, 128, 64, 128, 512
GLA_HEADS, GLA_DK, GLA_DV, GLA_GATE_RANK = 4, 64, 128, 16
GLA_GATE_NORM = 16.0
FOX_W = FOX_HEADS * FOX_HD
MLA_W = MLA_HEADS * MLA_V
GLA_W = GLA_HEADS * GLA_DV
MLA_QK = 2 * LANES
ROPE_BASE = 10000.0
LN_EPS = 1e-5
RMS_EPS = 1e-6
FOX_SCALE = FOX_HD ** -0.5
MLA_SCALE = (MLA_NOPE + MLA_ROPE) ** -0.5
GLA_SCALE = GLA_DK ** -0.5

IN_SIZES = (FOX_W, FOX_W, FOX_W, FOX_HEADS, FOX_W,
            MLA_HEADS * (MLA_NOPE + MLA_ROPE), MLA_KV_RANK, MLA_ROPE, MLA_W,
            GLA_HEADS * GLA_DK, GLA_HEADS * GLA_DK, GLA_W, GLA_GATE_RANK, GLA_W)
IN_OFFS = tuple(int(v) for v in np.concatenate([[0], np.cumsum(IN_SIZES)]))

PROJ_ROWS = 512
MERGE_ROWS = 256
ATTN_TQ = 1024
ATTN_TK = 1024
GLA_ROWS = 256
SAMPLE_TK = 1024


def _cparams(sem):
    return pltpu.CompilerParams(dimension_semantics=sem, vmem_limit_bytes=VMEM_LIMIT_BYTES)


def _row_tile(rows, want):
    t = min(rows, want)
    assert rows % t == 0, (rows, t)
    return t


def _split3(a):
    a1 = a.astype(BF16)
    r1 = a - a1.astype(F32)
    a2 = r1.astype(BF16)
    a3 = (r1 - a2.astype(F32)).astype(BF16)
    return a1, a2, a3


def _mask_dot(mask_bf16, a):
    a1, a2, a3 = _split3(a)
    d = functools.partial(jnp.dot, preferred_element_type=F32)
    return d(mask_bf16, a1) + d(mask_bf16, a2) + d(mask_bf16, a3)


def _dot_nt(a, b):
    return lax.dot_general(a, b, (((1,), (1,)), ((), ())), preferred_element_type=F32)


def _log_sigmoid(z):
    return jnp.minimum(z, 0.0) - jnp.log1p(jnp.exp(-jnp.abs(z)))


def _silu(z):
    return z / (1.0 + jnp.exp(-z))


def _iota(shape, dim):
    return lax.broadcasted_iota(jnp.int32, shape, dim)


def _div_pow2(x, n):
    assert n > 0 and n & (n - 1) == 0, n
    return lax.shift_right_logical(x, jnp.int32(n.bit_length() - 1))


def _fox_proj_kernel(x_ref, w_ref, bf_ref,
                     q_ref, kf_ref, vf_ref, kb_ref, vb_ref, g_ref, lf_ref, cc_ref, cr_ref,
                     carry_ref, *, seq, tiles_per_seq):
    tm = x_ref.shape[0]
    xb = x_ref[...].astype(BF16)

    def proj(a, b):
        return jnp.dot(xb, w_ref[:, a:b], preferred_element_type=F32)

    q_ref[...] = (proj(0, FOX_W) * FOX_SCALE).astype(BF16)
    k = proj(FOX_W, 2 * FOX_W)
    kf_ref[...] = k
    kb_ref[...] = k.astype(BF16)
    v = proj(2 * FOX_W, 3 * FOX_W)
    vf_ref[...] = v
    vb_ref[...] = v.astype(BF16)
    g_ref[...] = proj(3 * FOX_W, 4 * FOX_W)
    lf = _log_sigmoid(proj(4 * FOX_W, 4 * FOX_W + LANES) + bf_ref[...])
    lf_ref[...] = lf

    r = _iota((tm, tm), 0)
    c = _iota((tm, tm), 1)
    tri = r >= c
    if seq < tm:
        tri = jnp.logical_and(tri, _div_pow2(r, seq) == _div_pow2(c, seq))
    c_tile = _mask_dot(jnp.where(tri, 1.0, 0.0).astype(BF16), lf)
    if seq > tm:
        @pl.when(pl.program_id(0) % tiles_per_seq == 0)
        def _():
            carry_ref[...] = jnp.zeros_like(carry_ref)
        c_tile = c_tile + carry_ref[...]
        carry_ref[...] = c_tile[tm - 1:tm, :]
    cc_ref[...] = c_tile
    cr_ref[...] = c_tile.T[:SUBLANES, :]


def _fox_proj(x2, w, bf, seq):
    rows, d = x2.shape
    tm = _row_tile(rows, PROJ_ROWS)
    assert seq % tm == 0 or tm % seq == 0
    tiles_per_seq = max(seq // tm, 1)
    ncol = w.shape[1]
    row = lambda width: pl.BlockSpec((tm, width), lambda i: (i, 0))
    out_shape = (
        jax.ShapeDtypeStruct((rows, FOX_W), BF16),
        jax.ShapeDtypeStruct((rows, FOX_W), F32),
        jax.ShapeDtypeStruct((rows, FOX_W), F32),
        jax.ShapeDtypeStruct((rows, FOX_W), BF16),
        jax.ShapeDtypeStruct((rows, FOX_W), BF16),
        jax.ShapeDtypeStruct((rows, FOX_W), F32),
        jax.ShapeDtypeStruct((rows, LANES), F32),
        jax.ShapeDtypeStruct((rows, LANES), F32),
        jax.ShapeDtypeStruct((SUBLANES, rows), F32),
    )
    out_specs = (row(FOX_W), row(FOX_W), row(FOX_W), row(FOX_W), row(FOX_W), row(FOX_W),
                 row(LANES), row(LANES), pl.BlockSpec((SUBLANES, tm), lambda i: (0, i)))
    return pl.pallas_call(
        functools.partial(_fox_proj_kernel, seq=seq, tiles_per_seq=tiles_per_seq),
        out_shape=out_shape,
        grid=(rows // tm,),
        in_specs=[row(d),
                  pl.BlockSpec((d, ncol), lambda i: (0, 0)),
                  pl.BlockSpec((1, LANES), lambda i: (0, 0))],
        out_specs=out_specs,
        scratch_shapes=[pltpu.VMEM((1, LANES), F32)],
        compiler_params=_cparams(("arbitrary",)),
        name="fox_proj",
    )(x2, w, bf)


def _rope_block(blk, tab):
    t = blk * tab
    return t + pltpu.roll(t, LANES // 2, axis=1)


def _mla_proj_kernel(*refs, with_kv):
    if with_kv:
        (x_ref, w_ref, tab_ref, kvn_ref, wup_ref,
         q_ref, c_ref, kr_ref, g_ref, kc_ref, v_ref) = refs
    else:
        (x_ref, w_ref, tab_ref, kvn_ref,
         q_ref, c_ref, kr_ref, g_ref) = refs
    tm = x_ref.shape[0]
    xb = x_ref[...].astype(BF16)
    tab = tab_ref[...]

    def proj(a, b):
        return jnp.dot(xb, w_ref[:, a:b], preferred_element_type=F32)

    for h in range(MLA_HEADS):
        base = h * MLA_QK
        hq = proj(base, base + MLA_QK)
        q_ref[:, base:base + LANES] = (hq[:, :LANES] * MLA_SCALE).astype(BF16)
        q_ref[:, base + LANES:base + MLA_QK] = (_rope_block(hq[:, LANES:], tab) * MLA_SCALE).astype(BF16)
    off = MLA_HEADS * MLA_QK
    mc = proj(off, off + MLA_KV_RANK)
    c = mc * lax.rsqrt(jnp.mean(mc * mc, axis=-1, keepdims=True) + RMS_EPS) * kvn_ref[...]
    c_ref[...] = c
    off += MLA_KV_RANK
    kr = _rope_block(proj(off, off + LANES), tab)
    kr_ref[...] = kr
    off += LANES
    g_ref[...] = proj(off, off + MLA_W)
    if with_kv:
        cb = c.astype(BF16)
        krz = jnp.where(_iota((tm, LANES), 1) < MLA_ROPE, kr, 0.0).astype(BF16)
        for h in range(MLA_HEADS):
            kn = jnp.dot(cb, wup_ref[:, h * LANES:(h + 1) * LANES], preferred_element_type=F32)
            kc_ref[:, h * MLA_QK:h * MLA_QK + LANES] = kn.astype(BF16)
            kc_ref[:, h * MLA_QK + LANES:(h + 1) * MLA_QK] = krz
        v_ref[...] = jnp.dot(cb, wup_ref[:, MLA_W:], preferred_element_type=F32).astype(BF16)


def _mla_proj(x2, w, tab, kvn, wup, seq, with_kv):
    rows, d = x2.shape
    tm = _row_tile(rows, PROJ_ROWS)
    tab_tiles = tab.shape[0] // tm
    row = lambda width: pl.BlockSpec((tm, width), lambda i: (i, 0))
    const = lambda a: pl.BlockSpec(a.shape, lambda i: (0, 0))
    in_specs = [row(d), const(w),
                pl.BlockSpec((tm, LANES), lambda i: (i % tab_tiles, 0)),
                const(kvn)]
    args = [x2, w, tab, kvn]
    out_shape = [jax.ShapeDtypeStruct((rows, MLA_HEADS * MLA_QK), BF16),
                 jax.ShapeDtypeStruct((rows, MLA_KV_RANK), F32),
                 jax.ShapeDtypeStruct((rows, LANES), F32),
                 jax.ShapeDtypeStruct((rows, MLA_W), F32)]
    out_specs = [row(MLA_HEADS * MLA_QK), row(MLA_KV_RANK), row(LANES), row(MLA_W)]
    if with_kv:
        in_specs.append(const(wup))
        args.append(wup)
        out_shape += [jax.ShapeDtypeStruct((rows, MLA_HEADS * MLA_QK), BF16),
                      jax.ShapeDtypeStruct((rows, MLA_W), BF16)]
        out_specs += [row(MLA_HEADS * MLA_QK), row(MLA_W)]
    return pl.pallas_call(
        functools.partial(_mla_proj_kernel, with_kv=with_kv),
        out_shape=tuple(out_shape),
        grid=(rows // tm,),
        in_specs=in_specs,
        out_specs=tuple(out_specs),
        compiler_params=_cparams(("arbitrary",)),
        name="mla_proj_kv" if with_kv else "mla_proj",
    )(*args)


def _gla_proj_kernel(x_ref, w_ref, wgk_ref, bgk_ref, q_ref, k_ref, v_ref, gk_ref, g_ref):
    xb = x_ref[...].astype(BF16)

    def proj(a, b):
        return jnp.dot(xb, w_ref[:, a:b], preferred_element_type=F32)

    dk = GLA_HEADS * GLA_DK
    q_ref[...] = proj(0, dk)
    k_ref[...] = proj(dk, 2 * dk)
    v_ref[...] = proj(2 * dk, 2 * dk + GLA_W)
    off = 2 * dk + GLA_W
    low = proj(off, off + LANES).astype(BF16)
    z = jnp.dot(low, wgk_ref[...], preferred_element_type=F32) + bgk_ref[...]
    gk_ref[...] = _log_sigmoid(z) * (1.0 / GLA_GATE_NORM)
    off += LANES
    g_ref[...] = proj(off, off + GLA_W)


def _gla_proj(x2, w, wgk, bgk):
    rows, d = x2.shape
    tm = _row_tile(rows, PROJ_ROWS)
    dk = GLA_HEADS * GLA_DK
    row = lambda width: pl.BlockSpec((tm, width), lambda i: (i, 0))
    const = lambda a: pl.BlockSpec(a.shape, lambda i: (0, 0))
    return pl.pallas_call(
        _gla_proj_kernel,
        out_shape=(jax.ShapeDtypeStruct((rows, dk), F32),
                   jax.ShapeDtypeStruct((rows, dk), F32),
                   jax.ShapeDtypeStruct((rows, GLA_W), F32),
                   jax.ShapeDtypeStruct((rows, dk), F32),
                   jax.ShapeDtypeStruct((rows, GLA_W), F32)),
        grid=(rows // tm,),
        in_specs=[row(d), const(w), const(wgk), const(bgk)],
        out_specs=(row(dk), row(dk), row(GLA_W), row(dk), row(GLA_W)),
        compiler_params=_cparams(("arbitrary",)),
        name="gla_proj",
    )(x2, w, wgk, bgk)


def _softmax_step(s, v, m_ref, l_ref, acc_ref, h):
    lo, hi = h * LANES, (h + 1) * LANES
    m_prev = m_ref[h]
    m_next = jnp.maximum(m_prev, jnp.max(s, axis=1, keepdims=True))
    alpha = jnp.exp(m_prev - m_next)
    p = jnp.exp(s - m_next[:, :1])
    l_ref[h] = alpha * l_ref[h] + jnp.sum(p, axis=1, keepdims=True)
    acc_ref[:, lo:hi] = alpha * acc_ref[:, lo:hi] + jnp.dot(p.astype(BF16), v, preferred_element_type=F32)
    m_ref[h] = m_next


def _attn_init(m_ref, l_ref, acc_ref):
    m_ref[...] = jnp.full_like(m_ref, -jnp.inf)
    l_ref[...] = jnp.zeros_like(l_ref)
    acc_ref[...] = jnp.zeros_like(acc_ref)


def _attn_finish(o_ref, l_ref, acc_ref, heads):
    for h in range(heads):
        lo, hi = h * LANES, (h + 1) * LANES
        o_ref[:, lo:hi] = acc_ref[:, lo:hi] / l_ref[h]


def _prompt_attn_kernel(qi_ref, ki_ref, first_ref, last_ref, mask_ref, *refs, heads, qk, fox):
    if fox:
        q_ref, k_ref, v_ref, cc_ref, cr_ref, o_ref, m_ref, l_ref, acc_ref = refs
    else:
        q_ref, k_ref, v_ref, o_ref, m_ref, l_ref, acc_ref = refs
    s_idx = pl.program_id(1)
    tq, tk = q_ref.shape[0], k_ref.shape[0]

    @pl.when(first_ref[s_idx] == 1)
    def _():
        _attn_init(m_ref, l_ref, acc_ref)

    def tile(masked):
        if masked:
            qpos = qi_ref[s_idx] * tq + _iota((tq, tk), 0)
            kpos = ki_ref[s_idx] * tk + _iota((tq, tk), 1)
            if fox:
                keep = kpos <= qpos
            else:
                keep = _div_pow2(kpos, CHUNK) <= _div_pow2(qpos, CHUNK)
        for h in range(heads):
            s = _dot_nt(q_ref[:, h * qk:(h + 1) * qk], k_ref[:, h * qk:(h + 1) * qk])
            if fox:
                s = s + (cc_ref[:, h:h + 1] - cr_ref[h:h + 1, :])
            if masked:
                s = jnp.where(keep, s, -jnp.inf)
            _softmax_step(s, v_ref[:, h * LANES:(h + 1) * LANES], m_ref, l_ref, acc_ref, h)

    @pl.when(mask_ref[s_idx] == 1)
    def _():
        tile(True)

    @pl.when(mask_ref[s_idx] == 0)
    def _():
        tile(False)

    @pl.when(last_ref[s_idx] == 1)
    def _():
        _attn_finish(o_ref, l_ref, acc_ref, heads)


def _attn_schedule(seq, tq, tk, chunked):
    qi, ki, first, last, mask = [], [], [], [], []
    for a in range(seq // tq):
        q_lo, q_hi = a * tq, (a + 1) * tq - 1
        if chunked:
            q_lo, q_hi = (q_lo // CHUNK) * CHUNK, (q_hi // CHUNK) * CHUNK + CHUNK - 1
        tiles = [b for b in range(seq // tk) if b * tk <= q_hi]
        for b in tiles:
            qi.append(a)
            ki.append(b)
            first.append(int(b == tiles[0]))
            last.append(int(b == tiles[-1]))
            mask.append(int((b + 1) * tk - 1 > q_lo))
    return [jnp.asarray(np.asarray(t, np.int32)) for t in (qi, ki, first, last, mask)]


def _prompt_attn(q, k, v, batch, seq, *, heads, qk, fox, ccol=None, crow=None):
    tq = _row_tile(seq, ATTN_TQ)
    tk = _row_tile(seq, ATTN_TK)
    nq, nk = seq // tq, seq // tk
    tabs = _attn_schedule(seq, tq, tk, chunked=not fox)
    steps = int(tabs[0].shape[0])
    qmap = lambda b, s, qi, ki, *_: (b * nq + qi[s], 0)
    kmap = lambda b, s, qi, ki, *_: (b * nk + ki[s], 0)
    in_specs = [pl.BlockSpec((tq, heads * qk), qmap),
                pl.BlockSpec((tk, heads * qk), kmap),
                pl.BlockSpec((tk, heads * LANES), kmap)]
    args = [q, k, v]
    if fox:
        in_specs += [pl.BlockSpec((tq, LANES), qmap),
                     pl.BlockSpec((SUBLANES, tk), lambda b, s, qi, ki, *_: (0, b * nk + ki[s]))]
        args += [ccol, crow]
    return pl.pallas_call(
        functools.partial(_prompt_attn_kernel, heads=heads, qk=qk, fox=fox),
        out_shape=jax.ShapeDtypeStruct((batch * seq, heads * LANES), F32),
        grid_spec=pltpu.PrefetchScalarGridSpec(
            num_scalar_prefetch=5,
            grid=(batch, steps),
            in_specs=in_specs,
            out_specs=pl.BlockSpec((tq, heads * LANES), qmap),
            scratch_shapes=[pltpu.VMEM((heads, tq, LANES), F32),
                            pltpu.VMEM((heads, tq, LANES), F32),
                            pltpu.VMEM((tq, heads * LANES), F32)]),
        compiler_params=_cparams(("arbitrary", "arbitrary")),
        name="fox_attn" if fox else "mla_attn",
    )(*tabs, *args)


def _gla_kernel(q_ref, k_ref, g_ref, v_ref, s0_ref, o_ref, sf_ref, s_ref):
    t = pl.program_id(1)
    rows = q_ref.shape[0]
    blk = 2 * CHUNK
    pairs = GLA_HEADS // 2

    @pl.when(t == 0)
    def _():
        s_ref[...] = s0_ref[0]

    r = _iota((blk, blk), 0)
    c = _iota((blk, blk), 1)
    same_chunk = _div_pow2(r, CHUNK) == _div_pow2(c, CHUNK)
    tril = jnp.logical_and(same_chunk, r >= c)
    tril_b = jnp.where(tril, 1.0, 0.0).astype(BF16)
    ones_b = jnp.where(same_chunk, 1.0, 0.0).astype(BF16)
    lane_lo = c < CHUNK
    row_lo = r < CHUNK
    dot = functools.partial(jnp.dot, preferred_element_type=F32)

    for b0 in range(0, rows, blk):
        nchunk = min(blk, rows - b0) // CHUNK
        for hp in range(pairs):
            cols = slice(hp * LANES, (hp + 1) * LANES)
            qb = q_ref[b0:b0 + blk, cols] if nchunk == 2 else _pad_rows(q_ref[b0:b0 + CHUNK, cols], blk)
            kb = k_ref[b0:b0 + blk, cols] if nchunk == 2 else _pad_rows(k_ref[b0:b0 + CHUNK, cols], blk)
            gb = g_ref[b0:b0 + blk, cols] if nchunk == 2 else _pad_rows(g_ref[b0:b0 + CHUNK, cols], blk)
            bcum = _mask_dot(tril_b, gb)
            btot = _mask_dot(ones_b, gb)
            qe = qb * jnp.exp(bcum) * GLA_SCALE
            ke = kb * jnp.exp(-bcum)
            kt_t = (kb * jnp.exp(btot - bcum)).T
            btot_t = btot.T
            btot_sw = pltpu.roll(btot_t, CHUNK, axis=1)
            decay = (jnp.exp(jnp.where(lane_lo, btot_t, btot_sw)),
                     jnp.exp(jnp.where(lane_lo, btot_sw, btot_t)))
            keb = ke.astype(BF16)
            vbs, qms, intra = [], [], []
            for hh in range(2):
                h = 2 * hp + hh
                vcols = slice(h * LANES, (h + 1) * LANES)
                vb = v_ref[b0:b0 + blk, vcols] if nchunk == 2 else _pad_rows(v_ref[b0:b0 + CHUNK, vcols], blk)
                vb = vb.astype(BF16)
                qm = jnp.where(lane_lo if hh == 0 else jnp.logical_not(lane_lo), qe, 0.0).astype(BF16)
                a = jnp.where(tril, _dot_nt(qm, keb), 0.0)
                intra.append(dot(a.astype(BF16), vb))
                vbs.append(vb)
                qms.append(qm)
            state = s_ref[hp]
            outs = [[], []]
            for ch in range(nchunk):
                sb = state.astype(BF16)
                rs = slice(ch * CHUNK, (ch + 1) * CHUNK)
                for hh in range(2):
                    outs[hh].append(dot(qms[hh][rs], sb) + intra[hh][rs])
                kt_c = jnp.where(lane_lo if ch == 0 else jnp.logical_not(lane_lo), kt_t, 0.0).astype(BF16)
                upd = jnp.where(row_lo, dot(kt_c, vbs[0]), dot(kt_c, vbs[1]))
                state = decay[ch] * state + upd
            s_ref[hp] = state
            for hh in range(2):
                h = 2 * hp + hh
                res = outs[hh][0] if nchunk == 1 else jnp.concatenate(outs[hh], axis=0)
                o_ref[b0:b0 + nchunk * CHUNK, h * LANES:(h + 1) * LANES] = res

    @pl.when(t == pl.num_programs(1) - 1)
    def _():
        sf_ref[0] = s_ref[...]


def _pad_rows(a, rows):
    return jnp.concatenate([a, jnp.zeros((rows - a.shape[0], a.shape[1]), a.dtype)], axis=0)


def _gla(q, k, g, v, s0, batch, seq):
    tg = _row_tile(seq, GLA_ROWS)
    nt = seq // tg
    pairs = GLA_HEADS // 2
    dk = GLA_HEADS * GLA_DK
    row = lambda width: pl.BlockSpec((tg, width), lambda b, t: (b * nt + t, 0))
    state = pl.BlockSpec((1, pairs, LANES, LANES), lambda b, t: (b, 0, 0, 0))
    return pl.pallas_call(
        _gla_kernel,
        out_shape=(jax.ShapeDtypeStruct((batch * seq, GLA_W), F32),
                   jax.ShapeDtypeStruct((batch, pairs, LANES, LANES), F32)),
        grid=(batch, nt),
        in_specs=[row(dk), row(dk), row(dk), row(GLA_W), state],
        out_specs=(row(GLA_W), state),
        scratch_shapes=[pltpu.VMEM((pairs, LANES, LANES), F32)],
        compiler_params=_cparams(("arbitrary", "arbitrary")),
        name="gla_recurrence",
    )(q, k, g, v, s0)


def _merge_kernel(x_ref, of_ref, fg_ref, om_ref, mg_ref, og_ref, gg_ref, gn_ref, w_ref, lg_ref, lb_ref,
                  o_ref, *, alpha):
    dot = functools.partial(jnp.dot, preferred_element_type=F32)
    yf = (of_ref[...] * _silu(fg_ref[...])).astype(BF16)
    sub = dot(yf, w_ref[0:FOX_W, :])
    ym = (om_ref[...] * _silu(mg_ref[...])).astype(BF16)
    sub = sub + dot(ym, w_ref[FOX_W:FOX_W + MLA_W, :])
    for h in range(GLA_HEADS):
        lo, hi = h * GLA_DV, (h + 1) * GLA_DV
        og = og_ref[:, lo:hi]
        og = og * lax.rsqrt(jnp.mean(og * og, axis=-1, keepdims=True) + RMS_EPS) * gn_ref[...]
        yg = (og * _silu(gg_ref[:, lo:hi])).astype(BF16)
        base = FOX_W + MLA_W + lo
        sub = sub + dot(yg, w_ref[base:base + GLA_DV, :])
    z = alpha * x_ref[...] + sub
    mu = jnp.mean(z, axis=-1, keepdims=True)
    zc = z - mu
    var = jnp.mean(zc * zc, axis=-1, keepdims=True)
    o_ref[...] = zc * lax.rsqrt(var + LN_EPS) * lg_ref[...] + lb_ref[...]


def _merge(x2, o_fox, fg, o_mla, mg, o_gla, gg, gnorm, w_out, ln_g, ln_b, alpha):
    rows, d = x2.shape
    tm = _row_tile(rows, MERGE_ROWS)
    row = lambda width: pl.BlockSpec((tm, width), lambda i: (i, 0))
    const = lambda a: pl.BlockSpec(a.shape, lambda i: (0, 0))
    return pl.pallas_call(
        functools.partial(_merge_kernel, alpha=alpha),
        out_shape=jax.ShapeDtypeStruct((rows, d), F32),
        grid=(rows // tm,),
        in_specs=[row(d), row(FOX_W), row(FOX_W), row(MLA_W), row(MLA_W), row(GLA_W), row(GLA_W),
                  const(gnorm), const(w_out), const(ln_g), const(ln_b)],
        out_specs=row(d),
        compiler_params=_cparams(("arbitrary",)),
        name="merge",
    )(x2, o_fox, fg, o_mla, mg, o_gla, gg, gnorm, w_out, ln_g, ln_b)


def _past_cumsum_kernel(lf_ref, o_ref):
    past = lf_ref.shape[2]
    tl = min(past, 512)
    r = _iota((tl, tl), 0)
    c = _iota((tl, tl), 1)
    triu = jnp.where(r <= c, 1.0, 0.0).astype(BF16)
    d = functools.partial(jnp.dot, preferred_element_type=F32)
    carry = jnp.zeros((SUBLANES, 1), F32)
    for j in range(0, past, tl):
        a1, a2, a3 = _split3(lf_ref[0, :, j:j + tl])
        cum = d(a1, triu) + d(a2, triu) + d(a3, triu) + carry
        o_ref[0, :, j:j + tl] = cum
        carry = cum[:, tl - 1:tl]
    o_ref[0] = o_ref[0] - carry


def _past_cumsum(lf_rows):
    n, _, past = lf_rows.shape
    spec = pl.BlockSpec((1, SUBLANES, past), lambda i: (i, 0, 0))
    return pl.pallas_call(
        _past_cumsum_kernel,
        out_shape=jax.ShapeDtypeStruct(lf_rows.shape, F32),
        grid=(n,),
        in_specs=[spec],
        out_specs=spec,
        compiler_params=_cparams(("arbitrary",)),
        name="past_cumsum",
    )(lf_rows)


def _fox_sample_kernel(q_ref, ck_ref, cv_ref, cp_ref, kn_ref, vn_ref, cc_ref, cr_ref,
                       o_ref, m_ref, l_ref, acc_ref):
    j = pl.program_id(1)
    s_new = q_ref.shape[1]

    @pl.when(j == 0)
    def _():
        _attn_init(m_ref, l_ref, acc_ref)

    kt = ck_ref[0].astype(BF16)
    vt = cv_ref[0].astype(BF16)
    for h in range(FOX_HEADS):
        lo, hi = h * LANES, (h + 1) * LANES
        s = _dot_nt(q_ref[0, :, lo:hi], kt[:, lo:hi])
        s = s + (cc_ref[0, :, h:h + 1] - cp_ref[0, h:h + 1, :])
        _softmax_step(s, vt[:, lo:hi], m_ref, l_ref, acc_ref, h)

    @pl.when(j == pl.num_programs(1) - 1)
    def _():
        keep = _iota((s_new, s_new), 1) <= _iota((s_new, s_new), 0)
        for h in range(FOX_HEADS):
            lo, hi = h * LANES, (h + 1) * LANES
            s = _dot_nt(q_ref[0, :, lo:hi], kn_ref[0, :, lo:hi])
            s = s + (cc_ref[0, :, h:h + 1] - cr_ref[0, h:h + 1, :])
            s = jnp.where(keep, s, -jnp.inf)
            _softmax_step(s, vn_ref[0, :, lo:hi], m_ref, l_ref, acc_ref, h)
        _attn_finish(o_ref.at[0], l_ref, acc_ref, FOX_HEADS)


def _fox_sample(q, cache_k, cache_v, c_past, k_new, v_new, ccol, crow):
    batch, s_new, _ = q.shape
    past = cache_k.shape[1]
    tk = _row_tile(past, SAMPLE_TK)
    new = lambda width: pl.BlockSpec((1, s_new, width), lambda b, j: (b, 0, 0))
    cache = pl.BlockSpec((1, tk, FOX_W), lambda b, j: (b, j, 0))
    return pl.pallas_call(
        _fox_sample_kernel,
        out_shape=jax.ShapeDtypeStruct((batch, s_new, FOX_W), F32),
        grid=(batch, past // tk),
        in_specs=[new(FOX_W), cache, cache,
                  pl.BlockSpec((1, SUBLANES, tk), lambda b, j: (b, 0, j)),
                  new(FOX_W), new(FOX_W), new(LANES),
                  pl.BlockSpec((1, SUBLANES, s_new), lambda b, j: (b, 0, 0))],
        out_specs=new(FOX_W),
        scratch_shapes=[pltpu.VMEM((FOX_HEADS, s_new, LANES), F32),
                        pltpu.VMEM((FOX_HEADS, s_new, LANES), F32),
                        pltpu.VMEM((s_new, FOX_W), F32)],
        compiler_params=_cparams(("arbitrary", "arbitrary")),
        name="fox_sample_attn",
    )(q, cache_k, cache_v, c_past, k_new, v_new, ccol, crow)


def _mla_sample_kernel(q_ref, wuk_ref, wuv_ref, cc_ref, cr_ref, cn_ref, rn_ref,
                       o_ref, ql_ref, qr_ref, m_ref, l_ref, acc_ref):
    j = pl.program_id(1)
    s_new = q_ref.shape[1]
    dot = functools.partial(jnp.dot, preferred_element_type=F32)

    @pl.when(j == 0)
    def _():
        m_ref[...] = jnp.full_like(m_ref, -jnp.inf)
        l_ref[...] = jnp.zeros_like(l_ref)
        acc_ref[...] = jnp.zeros_like(acc_ref)
        for h in range(MLA_HEADS):
            base = h * MLA_QK
            rs = slice(h * s_new, (h + 1) * s_new)
            ql_ref[rs, :] = dot(q_ref[0, :, base:base + LANES], wuk_ref[h]).astype(BF16)
            qr_ref[rs, :] = q_ref[0, :, base + LANES:base + MLA_QK]

    def update(c_keys, r_keys):
        s = _dot_nt(ql_ref[...], c_keys) + _dot_nt(qr_ref[...], r_keys)
        m_prev = m_ref[...]
        m_next = jnp.maximum(m_prev, jnp.max(s, axis=1, keepdims=True))
        alpha = jnp.exp(m_prev - m_next)
        p = jnp.exp(s - m_next[:, :1])
        l_ref[...] = alpha * l_ref[...] + jnp.sum(p, axis=1, keepdims=True)
        acc_ref[...] = alpha[:, :1] * acc_ref[...] + dot(p.astype(BF16), c_keys)
        m_ref[...] = m_next

    rmask = _iota((1, LANES), 1) < MLA_ROPE
    update(cc_ref[0].astype(BF16), jnp.where(rmask, cr_ref[0], 0.0).astype(BF16))

    @pl.when(j == pl.num_programs(1) - 1)
    def _():
        update(cn_ref[0].astype(BF16), jnp.where(rmask, rn_ref[0], 0.0).astype(BF16))
        for h in range(MLA_HEADS):
            rs = slice(h * s_new, (h + 1) * s_new)
            o_lat = (acc_ref[rs, :] / l_ref[rs, :1]).astype(BF16)
            o_ref[0, :, h * LANES:(h + 1) * LANES] = dot(o_lat, wuv_ref[h])


def _mla_sample(q, wuk, wuv, cache_c, cache_r, c_new, r_new):
    batch, s_new, _ = q.shape
    past = cache_c.shape[1]
    tk = _row_tile(past, SAMPLE_TK)
    rows = MLA_HEADS * s_new
    new = lambda width: pl.BlockSpec((1, s_new, width), lambda b, j: (b, 0, 0))
    const = lambda a: pl.BlockSpec(a.shape, lambda b, j: (0, 0, 0))
    return pl.pallas_call(
        _mla_sample_kernel,
        out_shape=jax.ShapeDtypeStruct((batch, s_new, MLA_W), F32),
        grid=(batch, past // tk),
        in_specs=[new(MLA_HEADS * MLA_QK), const(wuk), const(wuv),
                  pl.BlockSpec((1, tk, MLA_KV_RANK), lambda b, j: (b, j, 0)),
                  pl.BlockSpec((1, tk, LANES), lambda b, j: (b, j, 0)),
                  new(MLA_KV_RANK), new(LANES)],
        out_specs=new(MLA_W),
        scratch_shapes=[pltpu.VMEM((rows, MLA_KV_RANK), BF16),
                        pltpu.VMEM((rows, LANES), BF16),
                        pltpu.VMEM((rows, LANES), F32),
                        pltpu.VMEM((rows, LANES), F32),
                        pltpu.VMEM((rows, MLA_KV_RANK), F32)],
        compiler_params=_cparams(("arbitrary", "arbitrary")),
        name="mla_sample_attn",
    )(q, wuk, wuv, cache_c, cache_r, c_new, r_new)


def _pad_cols(a, width):
    return jnp.pad(a, ((0, 0), (0, width - a.shape[1])))


def _swap_halves(a):
    half = a.shape[1] // 2
    return jnp.concatenate([a[:, half:], a[:, :half]], axis=1)


def _layer_params(w_in, b_f, kv_norm, w_up, w_gk_up, b_gk, gla_norm, w_out, ln_g, ln_b):
    seg = [w_in[:, IN_OFFS[i]:IN_OFFS[i + 1]] for i in range(len(IN_SIZES))]
    fq, fk, fv, ff, fg, mq, mc, mr, mg, gq, gk, gv, ggk, gg = seg
    w_fox = jnp.concatenate([fq, fk, fv, fg, _pad_cols(ff, LANES)], axis=1).astype(BF16)
    heads = []
    for h in range(MLA_HEADS):
        blk = mq[:, h * (MLA_NOPE + MLA_ROPE):(h + 1) * (MLA_NOPE + MLA_ROPE)]
        rope = blk[:, MLA_NOPE:]
        heads += [blk[:, :MLA_NOPE], rope, _swap_halves(rope)]
    w_mla = jnp.concatenate(heads + [mc, mr, _swap_halves(mr), mg], axis=1).astype(BF16)
    w_gla = jnp.concatenate([gq, gk, gv, _pad_cols(ggk, LANES), gg], axis=1).astype(BF16)
    w_up_k = w_up[:, :, :MLA_NOPE]
    w_up_v = w_up[:, :, MLA_NOPE:]
    return dict(
        w_fox=w_fox,
        b_f=_pad_cols(b_f[None, :], LANES),
        w_mla=w_mla,
        kv_norm=kv_norm[None, :],
        w_up=jnp.concatenate([w_up_k.reshape(MLA_KV_RANK, MLA_W),
                              w_up_v.reshape(MLA_KV_RANK, MLA_W)], axis=1).astype(BF16),
        w_uk=jnp.transpose(w_up_k, (1, 2, 0)).astype(BF16),
        w_uv=jnp.transpose(w_up_v, (1, 0, 2)).astype(BF16),
        w_gla=w_gla,
        w_gk=jnp.pad(w_gk_up, ((0, LANES - GLA_GATE_RANK), (0, 0))).astype(BF16),
        b_gk=b_gk[None, :],
        gla_norm=gla_norm[None, :],
        w_out=w_out.astype(BF16),
        ln_g=ln_g[None, :],
        ln_b=ln_b[None, :],
    )


def _rope_table(pos, rows):
    half = MLA_ROPE // 2
    inv = ROPE_BASE ** (-jnp.arange(half, dtype=F32) / half)
    ang = pos.astype(F32)[:, None] * inv[None, :]
    cos, sin = jnp.cos(ang), jnp.sin(ang)
    tab = jnp.concatenate([cos, cos, -sin, sin], axis=1)
    reps = max(rows // tab.shape[0], 1)
    return jnp.tile(tab, (reps, 1))


def _branches(x2, p, tab, seq, with_kv):
    fox = _fox_proj(x2, p["w_fox"], p["b_f"], seq)
    mla = _mla_proj(x2, p["w_mla"], tab, p["kv_norm"], p["w_up"], seq, with_kv)
    gla = _gla_proj(x2, p["w_gla"], p["w_gk"], p["b_gk"])
    return fox, mla, gla


def kernel(x_prompt, x_sample, cache_fox_k, cache_fox_v, cache_fox_logf, cache_mla_ckv, cache_mla_krope,
           state_gla, w_in, b_fox_f, mla_kv_norm, w_mla_kv_up, w_gla_gk_up, b_gla_gk, gla_norm, w_out,
           ln_g, ln_b):
    depth = w_in.shape[0]
    bp, tp, d = x_prompt.shape
    bs, ts, _ = x_sample.shape
    past = cache_fox_k.shape[2]
    alpha = (2 * depth) ** 0.25
    pairs = GLA_HEADS // 2

    rows_p = _row_tile(bp * tp, PROJ_ROWS)
    rows_s = _row_tile(bs * ts, PROJ_ROWS)
    tab_p = _rope_table(jnp.arange(tp), rows_p)
    tab_s = _rope_table(past + jnp.arange(ts), rows_s)

    xp = x_prompt.reshape(bp * tp, d)
    xs = x_sample.reshape(bs * ts, d)
    zero_state = jnp.zeros((bp, pairs, LANES, LANES), F32)
    outs = [[] for _ in range(12)]

    for l in range(depth):
        p = _layer_params(w_in[l], b_fox_f[l], mla_kv_norm[l], w_mla_kv_up[l], w_gla_gk_up[l],
                          b_gla_gk[l], gla_norm[l], w_out[l], ln_g[l], ln_b[l])

        (fq, fk, fv, fkb, fvb, fg, flf, fcc, fcr), (mq, mc, mkr, mg, mkc, mv), (gq, gk, gv, ggk, gg) = \
            _branches(xp, p, tab_p, tp, True)
        o_f = _prompt_attn(fq, fkb, fvb, bp, tp, heads=FOX_HEADS, qk=FOX_HD, fox=True, ccol=fcc, crow=fcr)
        o_m = _prompt_attn(mq, mkc, mv, bp, tp, heads=MLA_HEADS, qk=MLA_QK, fox=False)
        o_g, s_p = _gla(gq, gk, ggk, gv, zero_state, bp, tp)
        outs[0].append(fk.reshape(bp, tp, FOX_HEADS, FOX_HD))
        outs[1].append(fv.reshape(bp, tp, FOX_HEADS, FOX_HD))
        outs[2].append(flf[:, :FOX_HEADS].reshape(bp, tp, FOX_HEADS))
        outs[3].append(mc.reshape(bp, tp, MLA_KV_RANK))
        outs[4].append(mkr[:, :MLA_ROPE].reshape(bp, tp, MLA_ROPE))
        outs[5].append(s_p.reshape(bp, GLA_HEADS, GLA_DK, GLA_DV))
        xp = _merge(xp, o_f, fg, o_m, mg, o_g, gg, p["gla_norm"], p["w_out"], p["ln_g"], p["ln_b"], alpha)

        (fq, fk, fv, fkb, fvb, fg, flf, fcc, fcr), (mq, mc, mkr, mg), (gq, gk, gv, ggk, gg) = \
            _branches(xs, p, tab_s, ts, False)
        lf_rows = jnp.pad(jnp.swapaxes(cache_fox_logf[l].astype(F32), 1, 2),
                          ((0, 0), (0, SUBLANES - FOX_HEADS), (0, 0)))
        c_past = _past_cumsum(lf_rows)
        crow_s = jnp.swapaxes(fcr.reshape(SUBLANES, bs, ts), 0, 1)
        o_f = _fox_sample(fq.reshape(bs, ts, FOX_W),
                          cache_fox_k[l].reshape(bs, past, FOX_W), cache_fox_v[l].reshape(bs, past, FOX_W),
                          c_past, fkb.reshape(bs, ts, FOX_W), fvb.reshape(bs, ts, FOX_W),
                          fcc.reshape(bs, ts, LANES), crow_s)
        cache_r = jnp.pad(cache_mla_krope[l], ((0, 0), (0, 0), (0, LANES - MLA_ROPE)))
        o_m = _mla_sample(mq.reshape(bs, ts, MLA_HEADS * MLA_QK), p["w_uk"], p["w_uv"],
                          cache_mla_ckv[l], cache_r, mc.reshape(bs, ts, MLA_KV_RANK),
                          mkr.reshape(bs, ts, LANES))
        o_g, s_s = _gla(gq, gk, ggk, gv, state_gla[l].reshape(bs, pairs, LANES, LANES), bs, ts)
        outs[6].append(fk.reshape(bs, ts, FOX_HEADS, FOX_HD))
        outs[7].append(fv.reshape(bs, ts, FOX_HEADS, FOX_HD))
        outs[8].append(flf[:, :FOX_HEADS].reshape(bs, ts, FOX_HEADS))
        outs[9].append(mc.reshape(bs, ts, MLA_KV_RANK))
        outs[10].append(mkr[:, :MLA_ROPE].reshape(bs, ts, MLA_ROPE))
        outs[11].append(s_s.reshape(bs, GLA_HEADS, GLA_DK, GLA_DV))
        xs = _merge(xs, o_f.reshape(bs * ts, FOX_W), fg, o_m.reshape(bs * ts, MLA_W), mg, o_g, gg,
                    p["gla_norm"], p["w_out"], p["ln_g"], p["ln_b"], alpha)

    return (xp.reshape(bp, tp, d), xs.reshape(bs, ts, d)) + tuple(jnp.stack(o) for o in outs)
```

```python
import functools
import math

import numpy as np
import jax
import jax.numpy as jnp
from jax import lax
from jax.experimental import pallas as pl
from jax.experimental.pallas import tpu as pltpu

F32 = jnp.float32
BF16 = jnp.bfloat16

LANES = 128
SUBLANES = 8
VMEM_LIMIT_BYTES = 60 * 1024 * 1024

CHUNK = 64
FOX_HEADS, FOX_HD = 6, 128
MLA_HEADS, MLA_NOPE, MLA_ROPE, MLA_V, MLA_KV_RANK = 6, 128, 64, 128, 512
GLA_HEADS, GLA_DK, GLA_DV, GLA_GATE_RANK = 4, 64, 128, 16
GLA_GATE_NORM = 16.0
FOX_W = FOX_HEADS * FOX_HD
MLA_W = MLA_HEADS * MLA_V
GLA_W = GLA_HEADS * GLA_DV
QK = 2 * LANES
ROPE_BASE = 10000.0
LN_EPS = 1e-5
RMS_EPS = 1e-6
LOG2E = math.log2(math.e)
FOX_SCALE = FOX_HD ** -0.5
MLA_SCALE = (MLA_NOPE + MLA_ROPE) ** -0.5
GLA_SCALE = GLA_DK ** -0.5

IN_SIZES = (FOX_W, FOX_W, FOX_W, FOX_HEADS, FOX_W,
            MLA_HEADS * (MLA_NOPE + MLA_ROPE), MLA_KV_RANK, MLA_ROPE, MLA_W,
            GLA_HEADS * GLA_DK, GLA_HEADS * GLA_DK, GLA_W, GLA_GATE_RANK, GLA_W)
IN_OFFS = tuple(int(v) for v in np.concatenate([[0], np.cumsum(IN_SIZES)]))

PROJ_ROWS = 512
MERGE_ROWS = 256
ATTN_TQ = 1024
ATTN_TK = 1024
GLA_ROWS = 256
SAMPLE_TK = 1024


def _cparams(sem):
    return pltpu.CompilerParams(dimension_semantics=sem, vmem_limit_bytes=VMEM_LIMIT_BYTES)


def _row_tile(rows, want):
    t = min(rows, want)
    assert rows % t == 0, (rows, t)
    return t


def _split3(a):
    a1 = a.astype(BF16)
    r1 = a - a1.astype(F32)
    a2 = r1.astype(BF16)
    a3 = (r1 - a2.astype(F32)).astype(BF16)
    return a1, a2, a3


def _mask_dot(mask_bf16, a):
    a1, a2, a3 = _split3(a)
    d = functools.partial(jnp.dot, preferred_element_type=F32)
    return d(mask_bf16, a1) + d(mask_bf16, a2) + d(mask_bf16, a3)


def _dot_nt(a, b):
    return lax.dot_general(a, b, (((1,), (1,)), ((), ())), preferred_element_type=F32)


def _log_sigmoid(z):
    return jnp.minimum(z, 0.0) - jnp.log1p(jnp.exp(-jnp.abs(z)))


def _silu(z):
    return z / (1.0 + jnp.exp(-z))


def _iota(shape, dim):
    return lax.broadcasted_iota(jnp.int32, shape, dim)


def _div_pow2(x, n):
    assert n > 0 and n & (n - 1) == 0, n
    return lax.shift_right_logical(x, jnp.int32(n.bit_length() - 1))


def _lane_tile(a, width):
    return jnp.concatenate([a] * (width // LANES), axis=1)


def _fox_bias_selectors():
    sq = np.zeros((3 * LANES, FOX_W), np.float32)
    sk = np.zeros((3 * LANES, FOX_W), np.float32)
    oq = np.zeros((1, FOX_W), np.float32)
    ok = np.zeros((1, FOX_W), np.float32)
    for h in range(FOX_HEADS):
        for piece in range(3):
            sq[piece * LANES + h, h * LANES + piece] = 1.0
            sk[piece * LANES + h, h * LANES + 3 + piece] = -1.0
            oq[0, h * LANES + 3 + piece] = 1.0
            ok[0, h * LANES + piece] = 1.0
    return jnp.asarray(sq, BF16), jnp.asarray(sk, BF16), jnp.asarray(oq), jnp.asarray(ok)


def _fox_proj_kernel(x_ref, w_ref, bf_ref, sq_ref, sk_ref, oq_ref, ok_ref,
                     q_ref, k_ref, v_ref, kf_ref, vf_ref, g_ref, lf_ref, cc_ref, cr_ref,
                     carry_ref, *, seq, tiles_per_seq):
    tm = x_ref.shape[0]
    nb, _, tt, _ = kf_ref.shape
    xb = x_ref[...].astype(BF16)
    dot = functools.partial(jnp.dot, preferred_element_type=F32)

    def proj(a, b):
        return dot(xb, w_ref[:, a:b])

    q = proj(0, FOX_W) * (FOX_SCALE * LOG2E)
    k = proj(FOX_W, 2 * FOX_W)
    v = proj(2 * FOX_W, 3 * FOX_W)
    g_ref[...] = proj(3 * FOX_W, 4 * FOX_W)
    lf = _log_sigmoid(proj(4 * FOX_W, 4 * FOX_W + LANES) + bf_ref[...])
    lf_ref[...] = lf

    r = _iota((tm, tm), 0)
    c = _iota((tm, tm), 1)
    tri = r >= c
    if seq < tm:
        tri = jnp.logical_and(tri, _div_pow2(r, seq) == _div_pow2(c, seq))
    c_tile = _mask_dot(jnp.where(tri, 1.0, 0.0).astype(BF16), lf)
    if seq > tm:
        @pl.when(pl.program_id(0) % tiles_per_seq == 0)
        def _():
            carry_ref[...] = jnp.zeros_like(carry_ref)
        c_tile = c_tile + carry_ref[...]
        carry_ref[...] = c_tile[tm - 1:tm, :]
    c2 = c_tile * LOG2E
    cc_ref[...] = c2
    cr_ref[...] = c2.T[:SUBLANES, :]

    pieces = jnp.concatenate(_split3(c2), axis=1)
    qx = dot(pieces, sq_ref[...]) + oq_ref[...]
    kx = dot(pieces, sk_ref[...]) + ok_ref[...]
    for h in range(FOX_HEADS):
        cols = slice(h * LANES, (h + 1) * LANES)
        q_ref[h, :, :LANES] = q[:, cols].astype(BF16)
        q_ref[h, :, LANES:] = qx[:, cols].astype(BF16)
        k_ref[h, :, :LANES] = k[:, cols].astype(BF16)
        k_ref[h, :, LANES:] = kx[:, cols].astype(BF16)
        v_ref[h] = v[:, cols].astype(BF16)
        for b in range(nb):
            kf_ref[b, h] = k[b * tt:(b + 1) * tt, cols]
            vf_ref[b, h] = v[b * tt:(b + 1) * tt, cols]


def _fox_proj(x2, w, bf, sel, batch, seq):
    rows, d = x2.shape
    tm = _row_tile(rows, PROJ_ROWS)
    assert seq % tm == 0 or tm % seq == 0
    tiles_per_seq = max(seq // tm, 1)
    nb, tt = max(tm // seq, 1), min(tm, seq)
    row = lambda width: pl.BlockSpec((tm, width), lambda i: (i, 0))
    head = lambda width: pl.BlockSpec((FOX_HEADS, tm, width), lambda i: (0, i, 0))
    const = lambda a: pl.BlockSpec(a.shape, lambda i: (0, 0))
    cache = pl.BlockSpec((nb, FOX_HEADS, tt, FOX_HD),
                         lambda i: (i // tiles_per_seq, 0, i % tiles_per_seq, 0))
    out_shape = (
        jax.ShapeDtypeStruct((FOX_HEADS, rows, QK), BF16),
        jax.ShapeDtypeStruct((FOX_HEADS, rows, QK), BF16),
        jax.ShapeDtypeStruct((FOX_HEADS, rows, FOX_HD), BF16),
        jax.ShapeDtypeStruct((batch, FOX_HEADS, seq, FOX_HD), F32),
        jax.ShapeDtypeStruct((batch, FOX_HEADS, seq, FOX_HD), F32),
        jax.ShapeDtypeStruct((rows, FOX_W), F32),
        jax.ShapeDtypeStruct((rows, LANES), F32),
        jax.ShapeDtypeStruct((rows, LANES), F32),
        jax.ShapeDtypeStruct((SUBLANES, rows), F32),
    )
    out_specs = (head(QK), head(QK), head(FOX_HD), cache, cache, row(FOX_W), row(LANES), row(LANES),
                 pl.BlockSpec((SUBLANES, tm), lambda i: (0, i)))
    return pl.pallas_call(
        functools.partial(_fox_proj_kernel, seq=seq, tiles_per_seq=tiles_per_seq),
        out_shape=out_shape,
        grid=(rows // tm,),
        in_specs=[row(d), const(w), const(bf)] + [const(a) for a in sel],
        out_specs=out_specs,
        scratch_shapes=[pltpu.VMEM((1, LANES), F32)],
        compiler_params=_cparams(("arbitrary",)),
        name="fox_proj",
    )(x2, w, bf, *sel)


def _rope_block(blk, tab):
    t = blk * tab
    return t + pltpu.roll(t, LANES // 2, axis=1)


def _mla_proj_kernel(*refs, with_kv):
    if with_kv:
        (x_ref, w_ref, tab_ref, kvn_ref, wup_ref,
         q_ref, c_ref, kr_ref, kr2_ref, g_ref, kc_ref, v_ref) = refs
    else:
        (x_ref, w_ref, tab_ref, kvn_ref,
         q_ref, c_ref, kr_ref, kr2_ref, g_ref) = refs
    tm = x_ref.shape[0]
    xb = x_ref[...].astype(BF16)
    tab = tab_ref[...]
    dot = functools.partial(jnp.dot, preferred_element_type=F32)

    def proj(a, b):
        return dot(xb, w_ref[:, a:b])

    for h in range(MLA_HEADS):
        hq = proj(h * QK, (h + 1) * QK) * (MLA_SCALE * LOG2E)
        q_ref[h, :, :LANES] = hq[:, :LANES].astype(BF16)
        q_ref[h, :, LANES:] = _rope_block(hq[:, LANES:], tab).astype(BF16)
    off = MLA_HEADS * QK
    mc = proj(off, off + MLA_KV_RANK)
    c = mc * lax.rsqrt(jnp.mean(mc * mc, axis=-1, keepdims=True) + RMS_EPS) * kvn_ref[...]
    c_ref[...] = c
    off += MLA_KV_RANK
    kr = _rope_block(proj(off, off + LANES), tab)
    kr_ref[...] = kr[:, :MLA_ROPE]
    kr2_ref[...] = kr
    off += LANES
    g_ref[...] = proj(off, off + MLA_W)
    if with_kv:
        cb = c.astype(BF16)
        krz = jnp.where(_iota((tm, LANES), 1) < MLA_ROPE, kr, 0.0).astype(BF16)
        for h in range(MLA_HEADS):
            kc_ref[h, :, :LANES] = dot(cb, wup_ref[:, h * LANES:(h + 1) * LANES]).astype(BF16)
            kc_ref[h, :, LANES:] = krz
            v_ref[h] = dot(cb, wup_ref[:, MLA_W + h * LANES:MLA_W + (h + 1) * LANES]).astype(BF16)


def _mla_proj(x2, w, tab, kvn, wup, with_kv):
    rows, d = x2.shape
    tm = _row_tile(rows, PROJ_ROWS)
    tab_tiles = tab.shape[0] // tm
    row = lambda width: pl.BlockSpec((tm, width), lambda i: (i, 0))
    head = lambda width: pl.BlockSpec((MLA_HEADS, tm, width), lambda i: (0, i, 0))
    const = lambda a: pl.BlockSpec(a.shape, lambda i: (0, 0))
    in_specs = [row(d), const(w),
                pl.BlockSpec((tm, LANES), lambda i: (i % tab_tiles, 0)),
                const(kvn)]
    args = [x2, w, tab, kvn]
    out_shape = [jax.ShapeDtypeStruct((MLA_HEADS, rows, QK), BF16),
                 jax.ShapeDtypeStruct((rows, MLA_KV_RANK), F32),
                 jax.ShapeDtypeStruct((rows, MLA_ROPE), F32),
                 jax.ShapeDtypeStruct((rows, LANES), F32),
                 jax.ShapeDtypeStruct((rows, MLA_W), F32)]
    out_specs = [head(QK), row(MLA_KV_RANK), row(MLA_ROPE), row(LANES), row(MLA_W)]
    if with_kv:
        in_specs.append(const(wup))
        args.append(wup)
        out_shape += [jax.ShapeDtypeStruct((MLA_HEADS, rows, QK), BF16),
                      jax.ShapeDtypeStruct((MLA_HEADS, rows, MLA_V), BF16)]
        out_specs += [head(QK), head(MLA_V)]
    return pl.pallas_call(
        functools.partial(_mla_proj_kernel, with_kv=with_kv),
        out_shape=tuple(out_shape),
        grid=(rows // tm,),
        in_specs=in_specs,
        out_specs=tuple(out_specs),
        compiler_params=_cparams(("arbitrary",)),
        name="mla_proj_kv" if with_kv else "mla_proj",
    )(*args)


def _gla_proj_kernel(x_ref, w_ref, wgk_ref, bgk_ref, q_ref, k_ref, v_ref, gk_ref, g_ref):
    xb = x_ref[...].astype(BF16)

    def proj(a, b):
        return jnp.dot(xb, w_ref[:, a:b], preferred_element_type=F32)

    dk = GLA_HEADS * GLA_DK
    q_ref[...] = proj(0, dk)
    k_ref[...] = proj(dk, 2 * dk)
    v_ref[...] = proj(2 * dk, 2 * dk + GLA_W)
    off = 2 * dk + GLA_W
    low = proj(off, off + LANES).astype(BF16)
    z = jnp.dot(low, wgk_ref[...], preferred_element_type=F32) + bgk_ref[...]
    gk_ref[...] = _log_sigmoid(z) * (1.0 / GLA_GATE_NORM)
    off += LANES
    g_ref[...] = proj(off, off + GLA_W)


def _gla_proj(x2, w, wgk, bgk):
    rows, d = x2.shape
    tm = _row_tile(rows, PROJ_ROWS)
    dk = GLA_HEADS * GLA_DK
    row = lambda width: pl.BlockSpec((tm, width), lambda i: (i, 0))
    const = lambda a: pl.BlockSpec(a.shape, lambda i: (0, 0))
    return pl.pallas_call(
        _gla_proj_kernel,
        out_shape=(jax.ShapeDtypeStruct((rows, dk), F32),
                   jax.ShapeDtypeStruct((rows, dk), F32),
                   jax.ShapeDtypeStruct((rows, GLA_W), F32),
                   jax.ShapeDtypeStruct((rows, dk), F32),
                   jax.ShapeDtypeStruct((rows, GLA_W), F32)),
        grid=(rows // tm,),
        in_specs=[row(d), const(w), const(wgk), const(bgk)],
        out_specs=(row(dk), row(dk), row(GLA_W), row(dk), row(GLA_W)),
        compiler_params=_cparams(("arbitrary",)),
        name="gla_proj",
    )(x2, w, wgk, bgk)


def _online_softmax_update(s, v, m_prev, acc_prev):
    tk = s.shape[1]
    m_next = jnp.maximum(m_prev, jnp.max(s, axis=1, keepdims=True))
    alpha = jnp.exp2(m_prev - m_next)
    p = jnp.exp2(s - _lane_tile(m_next, tk)).astype(BF16)
    v_ones = jnp.concatenate([v, jnp.ones((tk, LANES), BF16)], axis=1)
    acc = _lane_tile(alpha, QK) * acc_prev + jnp.dot(p, v_ones, preferred_element_type=F32)
    return m_next, acc


def _prompt_attn_kernel(qi_ref, ki_ref, first_ref, last_ref, mask_ref,
                        q_ref, k_ref, v_ref, o_ref, m_ref, acc_ref, bias_ref, s_ref, *, chunked):
    s_idx = pl.program_id(1)
    heads, tq, _ = q_ref.shape
    tk = k_ref.shape[1]
    half = tq // 2
    lo, hi = slice(0, half), slice(half, tq)

    @pl.when(first_ref[s_idx] == 1)
    def _():
        m_ref[...] = jnp.full_like(m_ref, -jnp.inf)
        acc_ref[...] = jnp.zeros_like(acc_ref)

    def sweep(masked):
        def scores(h, rows, slot):
            s = _dot_nt(q_ref[h, rows, :], k_ref[h])
            if masked:
                s = s + bias_ref[rows, :]
            s_ref[slot] = s

        def softmax_pv(h, rows, slot):
            m_next, acc = _online_softmax_update(s_ref[slot], v_ref[h], m_ref[h, rows, :], acc_ref[h, rows, :])
            m_ref[h, rows, :] = m_next
            acc_ref[h, rows, :] = acc

        scores(0, lo, 0)

        def head(h, carry):
            scores(h, hi, 1)
            softmax_pv(h, lo, 0)
            scores(jnp.minimum(h + 1, heads - 1), lo, 0)
            softmax_pv(h, hi, 1)
            return carry
        lax.fori_loop(0, heads, head, 0, unroll=2)

    @pl.when(mask_ref[s_idx] == 1)
    def _():
        qpos = qi_ref[s_idx] * tq + _iota((tq, tk), 0)
        kpos = ki_ref[s_idx] * tk + _iota((tq, tk), 1)
        if chunked:
            keep = _div_pow2(kpos, CHUNK) <= _div_pow2(qpos, CHUNK)
        else:
            keep = kpos <= qpos
        bias_ref[...] = jnp.where(keep, 0.0, -jnp.inf)
        sweep(True)

    @pl.when(mask_ref[s_idx] == 0)
    def _():
        sweep(False)

    @pl.when(last_ref[s_idx] == 1)
    def _():
        for h in range(heads):
            o_ref[h] = acc_ref[h, :, :LANES] / acc_ref[h, :, LANES:]


def _attn_schedule(seq, tq, tk, chunked):
    qi, ki, first, last, mask = [], [], [], [], []
    for a in range(seq // tq):
        q_lo, q_hi = a * tq, (a + 1) * tq - 1
        if chunked:
            q_lo, q_hi = (q_lo // CHUNK) * CHUNK, (q_hi // CHUNK) * CHUNK + CHUNK - 1
        tiles = [b for b in range(seq // tk) if b * tk <= q_hi]
        for b in tiles:
            qi.append(a)
            ki.append(b)
            first.append(int(b == tiles[0]))
            last.append(int(b == tiles[-1]))
            mask.append(int((b + 1) * tk - 1 > q_lo))
    return [jnp.asarray(np.asarray(t, np.int32)) for t in (qi, ki, first, last, mask)]


def _prompt_attn(q, k, v, batch, seq, *, chunked):
    heads = q.shape[0]
    tq = _row_tile(seq, ATTN_TQ)
    tk = _row_tile(seq, ATTN_TK)
    nq, nk = seq // tq, seq // tk
    tabs = _attn_schedule(seq, tq, tk, chunked)
    steps = int(tabs[0].shape[0])
    qmap = lambda b, s, qi, ki, *_: (0, b * nq + qi[s], 0)
    kmap = lambda b, s, qi, ki, *_: (0, b * nk + ki[s], 0)
    return pl.pallas_call(
        functools.partial(_prompt_attn_kernel, chunked=chunked),
        out_shape=jax.ShapeDtypeStruct((heads, batch * seq, LANES), F32),
        grid_spec=pltpu.PrefetchScalarGridSpec(
            num_scalar_prefetch=5,
            grid=(batch, steps),
            in_specs=[pl.BlockSpec((heads, tq, QK), qmap),
                      pl.BlockSpec((heads, tk, QK), kmap),
                      pl.BlockSpec((heads, tk, LANES), kmap)],
            out_specs=pl.BlockSpec((heads, tq, LANES), qmap),
            scratch_shapes=[pltpu.VMEM((heads, tq, LANES), F32),
                            pltpu.VMEM((heads, tq, QK), F32),
                            pltpu.VMEM((tq, tk), F32),
                            pltpu.VMEM((2, tq // 2, tk), F32)]),
        compiler_params=_cparams(("arbitrary", "arbitrary")),
        name="mla_attn" if chunked else "fox_attn",
    )(*tabs, q, k, v)


def _gla_kernel(q_ref, k_ref, g_ref, v_ref, s0_ref, o_ref, sf_ref, s_ref):
    t = pl.program_id(1)
    rows = q_ref.shape[0]
    blk = 2 * CHUNK
    pairs = GLA_HEADS // 2

    @pl.when(t == 0)
    def _():
        s_ref[...] = s0_ref[0]

    r = _iota((blk, blk), 0)
    c = _iota((blk, blk), 1)
    same_chunk = _div_pow2(r, CHUNK) == _div_pow2(c, CHUNK)
    tril = jnp.logical_and(same_chunk, r >= c)
    tril_b = jnp.where(tril, 1.0, 0.0).astype(BF16)
    ones_b = jnp.where(same_chunk, 1.0, 0.0).astype(BF16)
    lane_lo = c < CHUNK
    row_lo = r < CHUNK
    dot = functools.partial(jnp.dot, preferred_element_type=F32)

    for b0 in range(0, rows, blk):
        nchunk = min(blk, rows - b0) // CHUNK
        for hp in range(pairs):
            cols = slice(hp * LANES, (hp + 1) * LANES)
            qb = q_ref[b0:b0 + blk, cols] if nchunk == 2 else _pad_rows(q_ref[b0:b0 + CHUNK, cols], blk)
            kb = k_ref[b0:b0 + blk, cols] if nchunk == 2 else _pad_rows(k_ref[b0:b0 + CHUNK, cols], blk)
            gb = g_ref[b0:b0 + blk, cols] if nchunk == 2 else _pad_rows(g_ref[b0:b0 + CHUNK, cols], blk)
            bcum = _mask_dot(tril_b, gb)
            btot = _mask_dot(ones_b, gb)
            qe = qb * jnp.exp(bcum) * GLA_SCALE
            ke = kb * jnp.exp(-bcum)
            kt_t = (kb * jnp.exp(btot - bcum)).T
            btot_t = btot.T
            btot_sw = pltpu.roll(btot_t, CHUNK, axis=1)
            decay = (jnp.exp(jnp.where(lane_lo, btot_t, btot_sw)),
                     jnp.exp(jnp.where(lane_lo, btot_sw, btot_t)))
            keb = ke.astype(BF16)
            vbs, qms, intra = [], [], []
            for hh in range(2):
                h = 2 * hp + hh
                vcols = slice(h * LANES, (h + 1) * LANES)
                vb = v_ref[b0:b0 + blk, vcols] if nchunk == 2 else _pad_rows(v_ref[b0:b0 + CHUNK, vcols], blk)
                vb = vb.astype(BF16)
                qm = jnp.where(lane_lo if hh == 0 else jnp.logical_not(lane_lo), qe, 0.0).astype(BF16)
                a = jnp.where(tril, _dot_nt(qm, keb), 0.0)
                intra.append(dot(a.astype(BF16), vb))
                vbs.append(vb)
                qms.append(qm)
            state = s_ref[hp]
            outs = [[], []]
            for ch in range(nchunk):
                sb = state.astype(BF16)
                rs = slice(ch * CHUNK, (ch + 1) * CHUNK)
                for hh in range(2):
                    outs[hh].append(dot(qms[hh][rs], sb) + intra[hh][rs])
                kt_c = jnp.where(lane_lo if ch == 0 else jnp.logical_not(lane_lo), kt_t, 0.0).astype(BF16)
                upd = jnp.where(row_lo, dot(kt_c, vbs[0]), dot(kt_c, vbs[1]))
                state = decay[ch] * state + upd
            s_ref[hp] = state
            for hh in range(2):
                h = 2 * hp + hh
                res = outs[hh][0] if nchunk == 1 else jnp.concatenate(outs[hh], axis=0)
                o_ref[b0:b0 + nchunk * CHUNK, h * LANES:(h + 1) * LANES] = res

    @pl.when(t == pl.num_programs(1) - 1)
    def _():
        sf_ref[0] = s_ref[...]


def _pad_rows(a, rows):
    return jnp.concatenate([a, jnp.zeros((rows - a.shape[0], a.shape[1]), a.dtype)], axis=0)


def _gla(q, k, g, v, s0, batch, seq):
    tg = _row_tile(seq, GLA_ROWS)
    nt = seq // tg
    pairs = GLA_HEADS // 2
    dk = GLA_HEADS * GLA_DK
    row = lambda width: pl.BlockSpec((tg, width), lambda b, t: (b * nt + t, 0))
    state = pl.BlockSpec((1, pairs, LANES, LANES), lambda b, t: (b, 0, 0, 0))
    return pl.pallas_call(
        _gla_kernel,
        out_shape=(jax.ShapeDtypeStruct((batch * seq, GLA_W), F32),
                   jax.ShapeDtypeStruct((batch, pairs, LANES, LANES), F32)),
        grid=(batch, nt),
        in_specs=[row(dk), row(dk), row(dk), row(GLA_W), state],
        out_specs=(row(GLA_W), state),
        scratch_shapes=[pltpu.VMEM((pairs, LANES, LANES), F32)],
        compiler_params=_cparams(("arbitrary", "arbitrary")),
        name="gla_recurrence",
    )(q, k, g, v, s0)


def _merge_kernel(x_ref, of_ref, fg_ref, om_ref, mg_ref, og_ref, gg_ref, gn_ref, w_ref, lg_ref, lb_ref,
                  o_ref, y_ref, *, alpha):
    for h in range(FOX_HEADS):
        cols = slice(h * LANES, (h + 1) * LANES)
        y_ref[:, cols] = (of_ref[h] * _silu(fg_ref[:, cols])).astype(BF16)
    for h in range(MLA_HEADS):
        cols = slice(h * LANES, (h + 1) * LANES)
        y_ref[:, FOX_W + h * LANES:FOX_W + (h + 1) * LANES] = (om_ref[h] * _silu(mg_ref[:, cols])).astype(BF16)
    for h in range(GLA_HEADS):
        cols = slice(h * GLA_DV, (h + 1) * GLA_DV)
        og = og_ref[:, cols]
        og = og * lax.rsqrt(jnp.mean(og * og, axis=-1, keepdims=True) + RMS_EPS) * gn_ref[...]
        base = FOX_W + MLA_W + h * GLA_DV
        y_ref[:, base:base + GLA_DV] = (og * _silu(gg_ref[:, cols])).astype(BF16)
    z = alpha * x_ref[...] + jnp.dot(y_ref[...], w_ref[...], preferred_element_type=F32)
    mu = jnp.mean(z, axis=-1, keepdims=True)
    zc = z - mu
    var = jnp.mean(zc * zc, axis=-1, keepdims=True)
    o_ref[...] = zc * lax.rsqrt(var + LN_EPS) * lg_ref[...] + lb_ref[...]


def _merge(x2, o_fox, fg, o_mla, mg, o_gla, gg, gnorm, w_out, ln_g, ln_b, alpha):
    rows, d = x2.shape
    tm = _row_tile(rows, MERGE_ROWS)
    row = lambda width: pl.BlockSpec((tm, width), lambda i: (i, 0))
    head = lambda n: pl.BlockSpec((n, tm, LANES), lambda i: (0, i, 0))
    const = lambda a: pl.BlockSpec(a.shape, lambda i: (0, 0))
    return pl.pallas_call(
        functools.partial(_merge_kernel, alpha=alpha),
        out_shape=jax.ShapeDtypeStruct((rows, d), F32),
        grid=(rows // tm,),
        in_specs=[row(d), head(FOX_HEADS), row(FOX_W), head(MLA_HEADS), row(MLA_W), row(GLA_W), row(GLA_W),
                  const(gnorm), const(w_out), const(ln_g), const(ln_b)],
        out_specs=row(d),
        scratch_shapes=[pltpu.VMEM((tm, d), BF16)],
        compiler_params=_cparams(("arbitrary",)),
        name="merge",
    )(x2, o_fox, fg, o_mla, mg, o_gla, gg, gnorm, w_out, ln_g, ln_b)


def _past_cumsum_kernel(lf_ref, o_ref):
    past = lf_ref.shape[2]
    tl = min(past, 512)
    r = _iota((tl, tl), 0)
    c = _iota((tl, tl), 1)
    triu = jnp.where(r <= c, 1.0, 0.0).astype(BF16)
    d = functools.partial(jnp.dot, preferred_element_type=F32)
    carry = jnp.zeros((SUBLANES, 1), F32)
    for j in range(0, past, tl):
        a1, a2, a3 = _split3(lf_ref[0, :, j:j + tl])
        cum = d(a1, triu) + d(a2, triu) + d(a3, triu) + carry
        o_ref[0, :, j:j + tl] = cum
        carry = cum[:, tl - 1:tl]
    o_ref[0] = (o_ref[0] - carry) * LOG2E


def _past_cumsum(lf_rows):
    n, _, past = lf_rows.shape
    spec = pl.BlockSpec((1, SUBLANES, past), lambda i: (i, 0, 0))
    return pl.pallas_call(
        _past_cumsum_kernel,
        out_shape=jax.ShapeDtypeStruct(lf_rows.shape, F32),
        grid=(n,),
        in_specs=[spec],
        out_specs=spec,
        compiler_params=_cparams(("arbitrary",)),
        name="past_cumsum",
    )(lf_rows)


def _sample_softmax_step(s, v, m_ref, l_ref, acc_ref, h):
    m_prev = m_ref[h]
    m_next = jnp.maximum(m_prev, jnp.max(s, axis=1, keepdims=True))
    alpha = jnp.exp2(m_prev - m_next)
    p = jnp.exp2(s - m_next[:, :1])
    l_ref[h] = alpha * l_ref[h] + jnp.sum(p, axis=1, keepdims=True)
    acc_ref[h] = alpha * acc_ref[h] + jnp.dot(p.astype(BF16), v, preferred_element_type=F32)
    m_ref[h] = m_next


def _fox_sample_kernel(q_ref, ck_ref, cv_ref, cp_ref, kn_ref, vn_ref, cc_ref, cr_ref,
                       o_ref, m_ref, l_ref, acc_ref):
    j = pl.program_id(1)
    s_new = q_ref.shape[1]

    @pl.when(j == 0)
    def _():
        m_ref[...] = jnp.full_like(m_ref, -jnp.inf)
        l_ref[...] = jnp.zeros_like(l_ref)
        acc_ref[...] = jnp.zeros_like(acc_ref)

    for h in range(FOX_HEADS):
        s = _dot_nt(q_ref[h, :, :LANES], ck_ref[0, 0, h].astype(BF16))
        s = s + (cc_ref[:, h:h + 1] - cp_ref[0, h:h + 1, :])
        _sample_softmax_step(s, cv_ref[0, 0, h].astype(BF16), m_ref, l_ref, acc_ref, h)

    @pl.when(j == pl.num_programs(1) - 1)
    def _():
        keep = _iota((s_new, s_new), 1) <= _iota((s_new, s_new), 0)
        for h in range(FOX_HEADS):
            s = _dot_nt(q_ref[h, :, :LANES], kn_ref[h, :, :LANES])
            s = s + (cc_ref[:, h:h + 1] - cr_ref[0, h:h + 1, :])
            s = jnp.where(keep, s, -jnp.inf)
            _sample_softmax_step(s, vn_ref[h], m_ref, l_ref, acc_ref, h)
            o_ref[h] = acc_ref[h] / l_ref[h]


def _fox_sample(q, cache_k, cache_v, c_past, k_new, v_new, ccol, crow, layer):
    _, batch, _, past, _ = cache_k.shape
    s_new = q.shape[1] // batch
    tk = _row_tile(past, SAMPLE_TK)
    new = lambda width: pl.BlockSpec((FOX_HEADS, s_new, width), lambda b, j: (0, b, 0))
    cache = pl.BlockSpec((1, 1, FOX_HEADS, tk, FOX_HD), lambda b, j: (layer, b, 0, j, 0))
    return pl.pallas_call(
        _fox_sample_kernel,
        out_shape=jax.ShapeDtypeStruct((FOX_HEADS, batch * s_new, FOX_HD), F32),
        grid=(batch, past // tk),
        in_specs=[new(QK), cache, cache,
                  pl.BlockSpec((1, SUBLANES, tk), lambda b, j: (layer * batch + b, 0, j)),
                  new(QK), new(FOX_HD),
                  pl.BlockSpec((s_new, LANES), lambda b, j: (b, 0)),
                  pl.BlockSpec((1, SUBLANES, s_new), lambda b, j: (b, 0, 0))],
        out_specs=new(FOX_HD),
        scratch_shapes=[pltpu.VMEM((FOX_HEADS, s_new, LANES), F32),
                        pltpu.VMEM((FOX_HEADS, s_new, LANES), F32),
                        pltpu.VMEM((FOX_HEADS, s_new, FOX_HD), F32)],
        compiler_params=_cparams(("arbitrary", "arbitrary")),
        name="fox_sample_attn",
    )(q, cache_k, cache_v, c_past, k_new, v_new, ccol, crow)


def _mla_sample_kernel(q_ref, wuk_ref, wuv_ref, cc_ref, cr_ref, cn_ref, rn_ref,
                       o_ref, ql_ref, qr_ref, m_ref, l_ref, acc_ref):
    j = pl.program_id(1)
    s_new = q_ref.shape[1]
    dot = functools.partial(jnp.dot, preferred_element_type=F32)

    @pl.when(j == 0)
    def _():
        m_ref[...] = jnp.full_like(m_ref, -jnp.inf)
        l_ref[...] = jnp.zeros_like(l_ref)
        acc_ref[...] = jnp.zeros_like(acc_ref)
        for h in range(MLA_HEADS):
            rs = slice(h * s_new, (h + 1) * s_new)
            ql_ref[rs, :] = dot(q_ref[h, :, :LANES], wuk_ref[h]).astype(BF16)
            qr_ref[rs, :] = q_ref[h, :, LANES:]

    def update(c_keys, r_keys):
        s = _dot_nt(ql_ref[...], c_keys) + _dot_nt(qr_ref[...], r_keys)
        m_prev = m_ref[...]
        m_next = jnp.maximum(m_prev, jnp.max(s, axis=1, keepdims=True))
        alpha = jnp.exp2(m_prev - m_next)
        p = jnp.exp2(s - m_next[:, :1])
        l_ref[...] = alpha * l_ref[...] + jnp.sum(p, axis=1, keepdims=True)
        acc_ref[...] = alpha[:, :1] * acc_ref[...] + dot(p.astype(BF16), c_keys)
        m_ref[...] = m_next

    update(cc_ref[0, 0].astype(BF16), cr_ref[0, 0].astype(BF16))

    @pl.when(j == pl.num_programs(1) - 1)
    def _():
        rn = jnp.where(_iota((s_new, LANES), 1) < MLA_ROPE, rn_ref[...], 0.0)
        update(cn_ref[...].astype(BF16), rn.astype(BF16))
        for h in range(MLA_HEADS):
            rs = slice(h * s_new, (h + 1) * s_new)
            o_lat = (acc_ref[rs, :] / l_ref[rs, :1]).astype(BF16)
            o_ref[h] = dot(o_lat, wuv_ref[h])


def _mla_sample(q, wuk, wuv, cache_c, cache_r, c_new, r_new, layer):
    _, batch, past, _ = cache_c.shape
    s_new = q.shape[1] // batch
    tk = _row_tile(past, SAMPLE_TK)
    rows = MLA_HEADS * s_new
    new = lambda width: pl.BlockSpec((s_new, width), lambda b, j: (b, 0))
    head = lambda width: pl.BlockSpec((MLA_HEADS, s_new, width), lambda b, j: (0, b, 0))
    const = lambda a: pl.BlockSpec(a.shape, lambda b, j: (0, 0, 0))
    return pl.pallas_call(
        _mla_sample_kernel,
        out_shape=jax.ShapeDtypeStruct((MLA_HEADS, batch * s_new, MLA_V), F32),
        grid=(batch, past // tk),
        in_specs=[head(QK), const(wuk), const(wuv),
                  pl.BlockSpec((1, 1, tk, MLA_KV_RANK), lambda b, j: (layer, b, j, 0)),
                  pl.BlockSpec((1, 1, tk, LANES), lambda b, j: (layer, b, j, 0)),
                  new(MLA_KV_RANK), new(LANES)],
        out_specs=head(MLA_V),
        scratch_shapes=[pltpu.VMEM((rows, MLA_KV_RANK), BF16),
                        pltpu.VMEM((rows, LANES), BF16),
                        pltpu.VMEM((rows, LANES), F32),
                        pltpu.VMEM((rows, LANES), F32),
                        pltpu.VMEM((rows, MLA_KV_RANK), F32)],
        compiler_params=_cparams(("arbitrary", "arbitrary")),
        name="mla_sample_attn",
    )(q, wuk, wuv, cache_c, cache_r, c_new, r_new)


def _pad_cols(a, width):
    return jnp.pad(a, ((0, 0), (0, width - a.shape[1])))


W_FOX_COLS = 4 * FOX_W + LANES
W_MLA_COLS = MLA_HEADS * QK + MLA_KV_RANK + LANES + MLA_W
W_GLA_COLS = 2 * GLA_HEADS * GLA_DK + GLA_W + LANES + GLA_W
W_IN_BLOCK_COLS = -(-IN_OFFS[-1] // LANES) * LANES
PACK_ROWS = 256


def _pack_kernel(w_ref, fox_ref, mla_ref, gla_ref):
    rows = w_ref.shape[1]
    lane = _iota((rows, LANES), 1)

    def take(src, width):
        start = (src // LANES) * LANES
        shift = src - start
        span = -(-(shift + width) // LANES) * LANES
        win = w_ref[0, :, start:start + span]
        if shift:
            win = pltpu.roll(win, span - shift, axis=1)
        return win[:, :width]

    def narrow(src, width):
        return jnp.where(lane < width, take(src, LANES), 0.0)

    def rope_pair(src):
        p = take(src, LANES)
        quarter = MLA_ROPE // 2
        return jnp.where(lane < MLA_ROPE, p,
                         jnp.where(lane < MLA_ROPE + quarter, pltpu.roll(p, quarter, axis=1),
                                   pltpu.roll(p, LANES - quarter, axis=1)))

    def put(ref, col, val):
        ref[0, :, col:col + val.shape[1]] = val.astype(BF16)

    fq, fk, fv, ff, fg, mq, mc, mr, mg, gq, gk, gv, ggk, gg = IN_OFFS[:-1]
    put(fox_ref, 0, take(fq, FOX_W))
    put(fox_ref, FOX_W, take(fk, FOX_W))
    put(fox_ref, 2 * FOX_W, take(fv, FOX_W))
    put(fox_ref, 3 * FOX_W, take(fg, FOX_W))
    put(fox_ref, 4 * FOX_W, narrow(ff, FOX_HEADS))
    for h in range(MLA_HEADS):
        src = mq + h * (MLA_NOPE + MLA_ROPE)
        put(mla_ref, h * QK, take(src, MLA_NOPE))
        put(mla_ref, h * QK + LANES, rope_pair(src + MLA_NOPE))
    col = MLA_HEADS * QK
    put(mla_ref, col, take(mc, MLA_KV_RANK))
    put(mla_ref, col + MLA_KV_RANK, rope_pair(mr))
    put(mla_ref, col + MLA_KV_RANK + LANES, take(mg, MLA_W))
    dk = GLA_HEADS * GLA_DK
    put(gla_ref, 0, take(gq, dk))
    put(gla_ref, dk, take(gk, dk))
    put(gla_ref, 2 * dk, take(gv, GLA_W))
    put(gla_ref, 2 * dk + GLA_W, narrow(ggk, GLA_GATE_RANK))
    put(gla_ref, 2 * dk + GLA_W + LANES, take(gg, GLA_W))


def _pack_w_in(w_in):
    depth, d, _ = w_in.shape
    tr = _row_tile(d, PACK_ROWS)
    out = lambda cols: pl.BlockSpec((1, tr, cols), lambda l, i: (l, i, 0))
    return pl.pallas_call(
        _pack_kernel,
        out_shape=(jax.ShapeDtypeStruct((depth, d, W_FOX_COLS), BF16),
                   jax.ShapeDtypeStruct((depth, d, W_MLA_COLS), BF16),
                   jax.ShapeDtypeStruct((depth, d, W_GLA_COLS), BF16)),
        grid=(depth, d // tr),
        in_specs=[pl.BlockSpec((1, tr, W_IN_BLOCK_COLS), lambda l, i: (l, i, 0))],
        out_specs=(out(W_FOX_COLS), out(W_MLA_COLS), out(W_GLA_COLS)),
        compiler_params=_cparams(("arbitrary", "arbitrary")),
        name="pack_w_in",
    )(w_in)


def _layer_params(w_fox, w_mla, w_gla, b_f, kv_norm, w_up, w_gk_up, b_gk, gla_norm, w_out, ln_g, ln_b):
    w_up_k = w_up[:, :, :MLA_NOPE]
    w_up_v = w_up[:, :, MLA_NOPE:]
    return dict(
        w_fox=w_fox,
        b_f=_pad_cols(b_f[None, :], LANES),
        w_mla=w_mla,
        kv_norm=kv_norm[None, :],
        w_up=jnp.concatenate([w_up_k.reshape(MLA_KV_RANK, MLA_W),
                              w_up_v.reshape(MLA_KV_RANK, MLA_W)], axis=1).astype(BF16),
        w_uk=jnp.transpose(w_up_k, (1, 2, 0)).astype(BF16),
        w_uv=jnp.transpose(w_up_v, (1, 0, 2)).astype(BF16),
        w_gla=w_gla,
        w_gk=jnp.pad(w_gk_up, ((0, LANES - GLA_GATE_RANK), (0, 0))).astype(BF16),
        b_gk=b_gk[None, :],
        gla_norm=gla_norm[None, :],
        w_out=w_out.astype(BF16),
        ln_g=ln_g[None, :],
        ln_b=ln_b[None, :],
    )


def _rope_table(pos, rows):
    half = MLA_ROPE // 2
    inv = ROPE_BASE ** (-jnp.arange(half, dtype=F32) / half)
    ang = pos.astype(F32)[:, None] * inv[None, :]
    cos, sin = jnp.cos(ang), jnp.sin(ang)
    tab = jnp.concatenate([cos, cos, -sin, sin], axis=1)
    reps = max(rows // tab.shape[0], 1)
    return jnp.tile(tab, (reps, 1))


def _branches(x2, p, sel, tab, batch, seq, with_kv):
    fox = _fox_proj(x2, p["w_fox"], p["b_f"], sel, batch, seq)
    mla = _mla_proj(x2, p["w_mla"], tab, p["kv_norm"], p["w_up"], with_kv)
    gla = _gla_proj(x2, p["w_gla"], p["w_gk"], p["b_gk"])
    return fox, mla, gla


def kernel(x_prompt, x_sample, cache_fox_k, cache_fox_v, cache_fox_logf, cache_mla_ckv, cache_mla_krope,
           state_gla, w_in, b_fox_f, mla_kv_norm, w_mla_kv_up, w_gla_gk_up, b_gla_gk, gla_norm, w_out,
           ln_g, ln_b):
    depth = w_in.shape[0]
    bp, tp, d = x_prompt.shape
    bs, ts, _ = x_sample.shape
    past = cache_fox_k.shape[2]
    alpha = (2 * depth) ** 0.25
    pairs = GLA_HEADS // 2

    tab_p = _rope_table(jnp.arange(tp), _row_tile(bp * tp, PROJ_ROWS))
    tab_s = _rope_table(past + jnp.arange(ts), _row_tile(bs * ts, PROJ_ROWS))
    sel = _fox_bias_selectors()

    cache_k = jnp.transpose(cache_fox_k, (0, 1, 3, 2, 4))
    cache_v = jnp.transpose(cache_fox_v, (0, 1, 3, 2, 4))
    lf_rows = jnp.pad(jnp.swapaxes(cache_fox_logf.astype(F32), 2, 3).reshape(depth * bs, FOX_HEADS, past),
                      ((0, 0), (0, SUBLANES - FOX_HEADS), (0, 0)))
    c_past = _past_cumsum(lf_rows)
    cache_r = jnp.pad(cache_mla_krope, ((0, 0), (0, 0), (0, 0), (0, LANES - MLA_ROPE)))

    xp = x_prompt.reshape(bp * tp, d)
    xs = x_sample.reshape(bs * ts, d)
    zero_state = jnp.zeros((bp, pairs, LANES, LANES), F32)
    outs = [[] for _ in range(12)]

    w_fox, w_mla, w_gla = _pack_w_in(w_in)

    for l in range(depth):
        p = _layer_params(w_fox[l], w_mla[l], w_gla[l], b_fox_f[l], mla_kv_norm[l], w_mla_kv_up[l], w_gla_gk_up[l],
                          b_gla_gk[l], gla_norm[l], w_out[l], ln_g[l], ln_b[l])

        (fq, fkc, fvb, fk, fv, fg, flf, _, _), (mq, mc, mkr, _, mg, mkc, mv), (gq, gk, gv, ggk, gg) = \
            _branches(xp, p, sel, tab_p, bp, tp, True)
        o_f = _prompt_attn(fq, fkc, fvb, bp, tp, chunked=False)
        o_m = _prompt_attn(mq, mkc, mv, bp, tp, chunked=True)
        o_g, s_p = _gla(gq, gk, ggk, gv, zero_state, bp, tp)
        outs[0].append(fk)
        outs[1].append(fv)
        outs[2].append(flf[:, :FOX_HEADS].reshape(bp, tp, FOX_HEADS))
        outs[3].append(mc.reshape(bp, tp, MLA_KV_RANK))
        outs[4].append(mkr.reshape(bp, tp, MLA_ROPE))
        outs[5].append(s_p.reshape(bp, GLA_HEADS, GLA_DK, GLA_DV))
        xp = _merge(xp, o_f, fg, o_m, mg, o_g, gg, p["gla_norm"], p["w_out"], p["ln_g"], p["ln_b"], alpha)

        (fq, fkc, fvb, fk, fv, fg, flf, fcc, fcr), (mq, mc, mkr, mkr2, mg), (gq, gk, gv, ggk, gg) = \
            _branches(xs, p, sel, tab_s, bs, ts, False)
        crow_s = jnp.swapaxes(fcr.reshape(SUBLANES, bs, ts), 0, 1)
        o_f = _fox_sample(fq, cache_k, cache_v, c_past, fkc, fvb, fcc, crow_s, l)
        o_m = _mla_sample(mq, p["w_uk"], p["w_uv"], cache_mla_ckv, cache_r, mc, mkr2, l)
        o_g, s_s = _gla(gq, gk, ggk, gv, state_gla[l].reshape(bs, pairs, LANES, LANES), bs, ts)
        outs[6].append(fk)
        outs[7].append(fv)
        outs[8].append(flf[:, :FOX_HEADS].reshape(bs, ts, FOX_HEADS))
        outs[9].append(mc.reshape(bs, ts, MLA_KV_RANK))
        outs[10].append(mkr.reshape(bs, ts, MLA_ROPE))
        outs[11].append(s_s.reshape(bs, GLA_HEADS, GLA_DK, GLA_DV))
        xs = _merge(xs, o_f, fg, o_m, mg, o_g, gg, p["gla_norm"], p["w_out"], p["ln_g"], p["ln_b"], alpha)

    stacked = [jnp.stack(o) for o in outs]
    for i in (0, 1, 6, 7):
        stacked[i] = jnp.transpose(stacked[i], (0, 1, 3, 2, 4))
    return (xp.reshape(bp, tp, d), xs.reshape(bs, ts, d)) + tuple(stacked)
```

```python
import functools
import math

import numpy as np
import jax
import jax.numpy as jnp
from jax import lax
from jax.experimental import pallas as pl
from jax.experimental.pallas import tpu as pltpu

F32 = jnp.float32
BF16 = jnp.bfloat16

LANES = 128
SUBLANES = 8
VMEM_LIMIT_BYTES = 60 * 1024 * 1024

CHUNK = 64
FOX_HEADS, FOX_HD = 6, 128
MLA_HEADS, MLA_NOPE, MLA_ROPE, MLA_V, MLA_KV_RANK = 6, 128, 64, 128, 512
GLA_HEADS, GLA_DK, GLA_DV, GLA_GATE_RANK = 4, 64, 128, 16
GLA_GATE_NORM = 16.0
FOX_W = FOX_HEADS * FOX_HD
MLA_W = MLA_HEADS * MLA_V
GLA_W = GLA_HEADS * GLA_DV
QK = 2 * LANES
ROPE_BASE = 10000.0
LN_EPS = 1e-5
RMS_EPS = 1e-6
LOG2E = math.log2(math.e)
FOX_SCALE = FOX_HD ** -0.5
MLA_SCALE = (MLA_NOPE + MLA_ROPE) ** -0.5
GLA_SCALE = GLA_DK ** -0.5

IN_SIZES = (FOX_W, FOX_W, FOX_W, FOX_HEADS, FOX_W,
            MLA_HEADS * (MLA_NOPE + MLA_ROPE), MLA_KV_RANK, MLA_ROPE, MLA_W,
            GLA_HEADS * GLA_DK, GLA_HEADS * GLA_DK, GLA_W, GLA_GATE_RANK, GLA_W)
IN_OFFS = tuple(int(v) for v in np.concatenate([[0], np.cumsum(IN_SIZES)]))

PROJ_ROWS = 512
MERGE_ROWS = 512
ATTN_TQ = 1024
ATTN_TK = 1024
GLA_ROWS = 256
SAMPLE_TK = 2048


def _cparams(sem):
    return pltpu.CompilerParams(dimension_semantics=sem, vmem_limit_bytes=VMEM_LIMIT_BYTES)


def _row_tile(rows, want):
    t = min(rows, want)
    assert rows % t == 0, (rows, t)
    return t


def _split3(a):
    a1 = a.astype(BF16)
    r1 = a - a1.astype(F32)
    a2 = r1.astype(BF16)
    a3 = (r1 - a2.astype(F32)).astype(BF16)
    return a1, a2, a3


def _mask_dot(mask_bf16, a):
    a1, a2, a3 = _split3(a)
    d = functools.partial(jnp.dot, preferred_element_type=F32)
    return d(mask_bf16, a1) + d(mask_bf16, a2) + d(mask_bf16, a3)


def _dot_nt(a, b):
    return lax.dot_general(a, b, (((1,), (1,)), ((), ())), preferred_element_type=F32)


def _log_sigmoid(z):
    return jnp.minimum(z, 0.0) - jnp.log1p(jnp.exp(-jnp.abs(z)))


def _silu(z):
    return z / (1.0 + jnp.exp(-z))


def _iota(shape, dim):
    return lax.broadcasted_iota(jnp.int32, shape, dim)


def _div_pow2(x, n):
    assert n > 0 and n & (n - 1) == 0, n
    return lax.shift_right_logical(x, jnp.int32(n.bit_length() - 1))


def _lane_tile(a, width):
    return jnp.concatenate([a] * (width // LANES), axis=1)


def _fox_bias_selectors():
    sq = np.zeros((3 * LANES, FOX_W), np.float32)
    sk = np.zeros((3 * LANES, FOX_W), np.float32)
    oq = np.zeros((1, FOX_W), np.float32)
    ok = np.zeros((1, FOX_W), np.float32)
    for h in range(FOX_HEADS):
        for piece in range(3):
            sq[piece * LANES + h, h * LANES + piece] = 1.0
            sk[piece * LANES + h, h * LANES + 3 + piece] = -1.0
            oq[0, h * LANES + 3 + piece] = 1.0
            ok[0, h * LANES + piece] = 1.0
    return jnp.asarray(sq, BF16), jnp.asarray(sk, BF16), jnp.asarray(oq), jnp.asarray(ok)


def _fox_proj_kernel(x_ref, w_ref, bf_ref, sq_ref, sk_ref, oq_ref, ok_ref,
                     q_ref, k_ref, v_ref, kf_ref, vf_ref, g_ref, lf_ref, cc_ref, cr_ref,
                     carry_ref, *, seq, tiles_per_seq):
    tm = x_ref.shape[0]
    nb, _, tt, _ = kf_ref.shape
    xb = x_ref[...].astype(BF16)
    dot = functools.partial(jnp.dot, preferred_element_type=F32)

    def proj(a, b):
        return dot(xb, w_ref[:, a:b])

    q = proj(0, FOX_W) * (FOX_SCALE * LOG2E)
    k = proj(FOX_W, 2 * FOX_W)
    v = proj(2 * FOX_W, 3 * FOX_W)
    g_ref[...] = proj(3 * FOX_W, 4 * FOX_W).astype(BF16)
    lf = _log_sigmoid(proj(4 * FOX_W, 4 * FOX_W + LANES) + bf_ref[...])
    lf_ref[...] = lf

    r = _iota((tm, tm), 0)
    c = _iota((tm, tm), 1)
    tri = r >= c
    if seq < tm:
        tri = jnp.logical_and(tri, _div_pow2(r, seq) == _div_pow2(c, seq))
    c_tile = _mask_dot(jnp.where(tri, 1.0, 0.0).astype(BF16), lf)
    if seq > tm:
        @pl.when(pl.program_id(0) % tiles_per_seq == 0)
        def _():
            carry_ref[...] = jnp.zeros_like(carry_ref)
        c_tile = c_tile + carry_ref[...]
        carry_ref[...] = c_tile[tm - 1:tm, :]
    c2 = c_tile * LOG2E
    cc_ref[...] = c2
    cr_ref[...] = c2.T[:SUBLANES, :]

    pieces = jnp.concatenate(_split3(c2), axis=1)
    qx = dot(pieces, sq_ref[...]) + oq_ref[...]
    kx = dot(pieces, sk_ref[...]) + ok_ref[...]
    for h in range(FOX_HEADS):
        cols = slice(h * LANES, (h + 1) * LANES)
        q_ref[h, :, :LANES] = q[:, cols].astype(BF16)
        q_ref[h, :, LANES:] = qx[:, cols].astype(BF16)
        k_ref[h, :, :LANES] = k[:, cols].astype(BF16)
        k_ref[h, :, LANES:] = kx[:, cols].astype(BF16)
        v_ref[h] = v[:, cols].astype(BF16)
        for b in range(nb):
            kf_ref[b, h] = k[b * tt:(b + 1) * tt, cols]
            vf_ref[b, h] = v[b * tt:(b + 1) * tt, cols]


def _fox_proj(x2, w, bf, sel, batch, seq):
    rows, d = x2.shape
    tm = _row_tile(rows, PROJ_ROWS)
    assert seq % tm == 0 or tm % seq == 0
    tiles_per_seq = max(seq // tm, 1)
    nb, tt = max(tm // seq, 1), min(tm, seq)
    row = lambda width: pl.BlockSpec((tm, width), lambda i: (i, 0))
    head = lambda width: pl.BlockSpec((FOX_HEADS, tm, width), lambda i: (0, i, 0))
    const = lambda a: pl.BlockSpec(a.shape, lambda i: (0, 0))
    cache = pl.BlockSpec((nb, FOX_HEADS, tt, FOX_HD),
                         lambda i: (i // tiles_per_seq, 0, i % tiles_per_seq, 0))
    out_shape = (
        jax.ShapeDtypeStruct((FOX_HEADS, rows, QK), BF16),
        jax.ShapeDtypeStruct((FOX_HEADS, rows, QK), BF16),
        jax.ShapeDtypeStruct((FOX_HEADS, rows, FOX_HD), BF16),
        jax.ShapeDtypeStruct((batch, FOX_HEADS, seq, FOX_HD), F32),
        jax.ShapeDtypeStruct((batch, FOX_HEADS, seq, FOX_HD), F32),
        jax.ShapeDtypeStruct((rows, FOX_W), BF16),
        jax.ShapeDtypeStruct((rows, LANES), F32),
        jax.ShapeDtypeStruct((rows, LANES), F32),
        jax.ShapeDtypeStruct((SUBLANES, rows), F32),
    )
    out_specs = (head(QK), head(QK), head(FOX_HD), cache, cache, row(FOX_W), row(LANES), row(LANES),
                 pl.BlockSpec((SUBLANES, tm), lambda i: (0, i)))
    return pl.pallas_call(
        functools.partial(_fox_proj_kernel, seq=seq, tiles_per_seq=tiles_per_seq),
        out_shape=out_shape,
        grid=(rows // tm,),
        in_specs=[row(d), const(w), const(bf)] + [const(a) for a in sel],
        out_specs=out_specs,
        scratch_shapes=[pltpu.VMEM((1, LANES), F32)],
        compiler_params=_cparams(("arbitrary",)),
        name="fox_proj",
    )(x2, w, bf, *sel)


def _rope_block(blk, tab):
    t = blk * tab
    return t + pltpu.roll(t, LANES // 2, axis=1)


def _mla_proj_kernel(*refs, with_kv):
    if with_kv:
        (x_ref, w_ref, tab_ref, kvn_ref, wup_ref,
         q_ref, c_ref, kr_ref, kr2_ref, g_ref, kc_ref, v_ref) = refs
    else:
        (x_ref, w_ref, tab_ref, kvn_ref,
         q_ref, c_ref, kr_ref, kr2_ref, g_ref) = refs
    tm = x_ref.shape[0]
    xb = x_ref[...].astype(BF16)
    tab = tab_ref[...]
    dot = functools.partial(jnp.dot, preferred_element_type=F32)

    def proj(a, b):
        return dot(xb, w_ref[:, a:b])

    for h in range(MLA_HEADS):
        hq = proj(h * QK, (h + 1) * QK) * (MLA_SCALE * LOG2E)
        q_ref[h, :, :LANES] = hq[:, :LANES].astype(BF16)
        q_ref[h, :, LANES:] = _rope_block(hq[:, LANES:], tab).astype(BF16)
    off = MLA_HEADS * QK
    mc = proj(off, off + MLA_KV_RANK)
    c = mc * lax.rsqrt(jnp.mean(mc * mc, axis=-1, keepdims=True) + RMS_EPS) * kvn_ref[...]
    c_ref[...] = c
    off += MLA_KV_RANK
    kr = _rope_block(proj(off, off + LANES), tab)
    kr_ref[...] = kr[:, :MLA_ROPE]
    kr2_ref[...] = kr
    off += LANES
    g_ref[...] = proj(off, off + MLA_W).astype(BF16)
    if with_kv:
        cb = c.astype(BF16)
        krz = jnp.where(_iota((tm, LANES), 1) < MLA_ROPE, kr, 0.0).astype(BF16)
        for h in range(MLA_HEADS):
            kc_ref[h, :, :LANES] = dot(cb, wup_ref[:, h * LANES:(h + 1) * LANES]).astype(BF16)
            kc_ref[h, :, LANES:] = krz
            v_ref[h] = dot(cb, wup_ref[:, MLA_W + h * LANES:MLA_W + (h + 1) * LANES]).astype(BF16)


def _mla_proj(x2, w, tab, kvn, wup, with_kv):
    rows, d = x2.shape
    tm = _row_tile(rows, PROJ_ROWS)
    tab_tiles = tab.shape[0] // tm
    row = lambda width: pl.BlockSpec((tm, width), lambda i: (i, 0))
    head = lambda width: pl.BlockSpec((MLA_HEADS, tm, width), lambda i: (0, i, 0))
    const = lambda a: pl.BlockSpec(a.shape, lambda i: (0, 0))
    in_specs = [row(d), const(w),
                pl.BlockSpec((tm, LANES), lambda i: (i % tab_tiles, 0)),
                const(kvn)]
    args = [x2, w, tab, kvn]
    out_shape = [jax.ShapeDtypeStruct((MLA_HEADS, rows, QK), BF16),
                 jax.ShapeDtypeStruct((rows, MLA_KV_RANK), F32),
                 jax.ShapeDtypeStruct((rows, MLA_ROPE), F32),
                 jax.ShapeDtypeStruct((rows, LANES), F32),
                 jax.ShapeDtypeStruct((rows, MLA_W), BF16)]
    out_specs = [head(QK), row(MLA_KV_RANK), row(MLA_ROPE), row(LANES), row(MLA_W)]
    if with_kv:
        in_specs.append(const(wup))
        args.append(wup)
        out_shape += [jax.ShapeDtypeStruct((MLA_HEADS, rows, QK), BF16),
                      jax.ShapeDtypeStruct((MLA_HEADS, rows, MLA_V), BF16)]
        out_specs += [head(QK), head(MLA_V)]
    return pl.pallas_call(
        functools.partial(_mla_proj_kernel, with_kv=with_kv),
        out_shape=tuple(out_shape),
        grid=(rows // tm,),
        in_specs=in_specs,
        out_specs=tuple(out_specs),
        compiler_params=_cparams(("arbitrary",)),
        name="mla_proj_kv" if with_kv else "mla_proj",
    )(*args)


def _gla_proj_kernel(x_ref, w_ref, wgk_ref, bgk_ref, q_ref, k_ref, v_ref, gk_ref, g_ref):
    xb = x_ref[...].astype(BF16)

    def proj(a, b):
        return jnp.dot(xb, w_ref[:, a:b], preferred_element_type=F32)

    dk = GLA_HEADS * GLA_DK
    q_ref[...] = proj(0, dk)
    k_ref[...] = proj(dk, 2 * dk)
    v_ref[...] = proj(2 * dk, 2 * dk + GLA_W)
    off = 2 * dk + GLA_W
    low = proj(off, off + LANES).astype(BF16)
    z = jnp.dot(low, wgk_ref[...], preferred_element_type=F32) + bgk_ref[...]
    gk_ref[...] = _log_sigmoid(z) * (1.0 / GLA_GATE_NORM)
    off += LANES
    g_ref[...] = proj(off, off + GLA_W).astype(BF16)


def _gla_proj(x2, w, wgk, bgk):
    rows, d = x2.shape
    tm = _row_tile(rows, PROJ_ROWS)
    dk = GLA_HEADS * GLA_DK
    row = lambda width: pl.BlockSpec((tm, width), lambda i: (i, 0))
    const = lambda a: pl.BlockSpec(a.shape, lambda i: (0, 0))
    return pl.pallas_call(
        _gla_proj_kernel,
        out_shape=(jax.ShapeDtypeStruct((rows, dk), F32),
                   jax.ShapeDtypeStruct((rows, dk), F32),
                   jax.ShapeDtypeStruct((rows, GLA_W), F32),
                   jax.ShapeDtypeStruct((rows, dk), F32),
                   jax.ShapeDtypeStruct((rows, GLA_W), BF16)),
        grid=(rows // tm,),
        in_specs=[row(d), const(w), const(wgk), const(bgk)],
        out_specs=(row(dk), row(dk), row(GLA_W), row(dk), row(GLA_W)),
        compiler_params=_cparams(("arbitrary",)),
        name="gla_proj",
    )(x2, w, wgk, bgk)


def _online_softmax_update(s, v, m_prev, acc_prev):
    tk = s.shape[1]
    m_next = jnp.maximum(m_prev, jnp.max(s, axis=1, keepdims=True))
    alpha = jnp.exp2(m_prev - m_next)
    p = jnp.exp2(s - _lane_tile(m_next, tk)).astype(BF16)
    v_ones = jnp.concatenate([v, jnp.ones((tk, LANES), BF16)], axis=1)
    acc = _lane_tile(alpha, QK) * acc_prev + jnp.dot(p, v_ones, preferred_element_type=F32)
    return m_next, acc


def _prompt_attn_kernel(qi_ref, ki_ref, first_ref, last_ref, mask_ref,
                        q_ref, k_ref, v_ref, o_ref, m_ref, acc_ref, bias_ref, s_ref, *, chunked):
    s_idx = pl.program_id(1)
    heads, tq, _ = q_ref.shape
    tk = k_ref.shape[1]
    half = tq // 2
    lo, hi = slice(0, half), slice(half, tq)

    @pl.when(first_ref[s_idx] == 1)
    def _():
        m_ref[...] = jnp.full_like(m_ref, -jnp.inf)
        acc_ref[...] = jnp.zeros_like(acc_ref)

    def sweep(masked):
        def scores(h, rows, slot):
            s = _dot_nt(q_ref[h, rows, :], k_ref[h])
            if masked:
                s = s + bias_ref[rows, :]
            s_ref[slot] = s

        def softmax_pv(h, rows, slot):
            m_next, acc = _online_softmax_update(s_ref[slot], v_ref[h], m_ref[h, rows, :], acc_ref[h, rows, :])
            m_ref[h, rows, :] = m_next
            acc_ref[h, rows, :] = acc

        scores(0, lo, 0)
        for h in range(heads):
            scores(h, hi, 1)
            softmax_pv(h, lo, 0)
            if h + 1 < heads:
                scores(h + 1, lo, 0)
            softmax_pv(h, hi, 1)

    @pl.when(mask_ref[s_idx] == 1)
    def _():
        qpos = qi_ref[s_idx] * tq + _iota((tq, tk), 0)
        kpos = ki_ref[s_idx] * tk + _iota((tq, tk), 1)
        if chunked:
            keep = _div_pow2(kpos, CHUNK) <= _div_pow2(qpos, CHUNK)
        else:
            keep = kpos <= qpos
        bias_ref[...] = jnp.where(keep, 0.0, -jnp.inf)
        sweep(True)

    @pl.when(mask_ref[s_idx] == 0)
    def _():
        sweep(False)

    @pl.when(last_ref[s_idx] == 1)
    def _():
        for h in range(heads):
            o_ref[h] = (acc_ref[h, :, :LANES] / acc_ref[h, :, LANES:]).astype(BF16)


def _attn_schedule(seq, tq, tk, chunked):
    qi, ki, first, last, mask = [], [], [], [], []
    for a in range(seq // tq):
        q_lo, q_hi = a * tq, (a + 1) * tq - 1
        if chunked:
            q_lo, q_hi = (q_lo // CHUNK) * CHUNK, (q_hi // CHUNK) * CHUNK + CHUNK - 1
        tiles = [b for b in range(seq // tk) if b * tk <= q_hi]
        for b in tiles:
            qi.append(a)
            ki.append(b)
            first.append(int(b == tiles[0]))
            last.append(int(b == tiles[-1]))
            mask.append(int((b + 1) * tk - 1 > q_lo))
    return [jnp.asarray(np.asarray(t, np.int32)) for t in (qi, ki, first, last, mask)]


def _prompt_attn(q, k, v, batch, seq, *, chunked):
    heads = q.shape[0]
    tq = _row_tile(seq, ATTN_TQ)
    tk = _row_tile(seq, ATTN_TK)
    nq, nk = seq // tq, seq // tk
    tabs = _attn_schedule(seq, tq, tk, chunked)
    steps = int(tabs[0].shape[0])
    qmap = lambda b, s, qi, ki, *_: (0, b * nq + qi[s], 0)
    kmap = lambda b, s, qi, ki, *_: (0, b * nk + ki[s], 0)
    return pl.pallas_call(
        functools.partial(_prompt_attn_kernel, chunked=chunked),
        out_shape=jax.ShapeDtypeStruct((heads, batch * seq, LANES), BF16),
        grid_spec=pltpu.PrefetchScalarGridSpec(
            num_scalar_prefetch=5,
            grid=(batch, steps),
            in_specs=[pl.BlockSpec((heads, tq, QK), qmap),
                      pl.BlockSpec((heads, tk, QK), kmap),
                      pl.BlockSpec((heads, tk, LANES), kmap)],
            out_specs=pl.BlockSpec((heads, tq, LANES), qmap),
            scratch_shapes=[pltpu.VMEM((heads, tq, LANES), F32),
                            pltpu.VMEM((heads, tq, QK), F32),
                            pltpu.VMEM((tq, tk), F32),
                            pltpu.VMEM((2, tq // 2, tk), F32)]),
        compiler_params=_cparams(("arbitrary", "arbitrary")),
        name="mla_attn" if chunked else "fox_attn",
    )(*tabs, q, k, v)


def _gla_kernel(q_ref, k_ref, g_ref, v_ref, s0_ref, o_ref, sf_ref, s_ref):
    t = pl.program_id(1)
    rows = q_ref.shape[0]
    blk = 2 * CHUNK
    pairs = GLA_HEADS // 2

    @pl.when(t == 0)
    def _():
        s_ref[...] = s0_ref[0]

    r = _iota((blk, blk), 0)
    c = _iota((blk, blk), 1)
    same_chunk = _div_pow2(r, CHUNK) == _div_pow2(c, CHUNK)
    tril = jnp.logical_and(same_chunk, r >= c)
    tril_b = jnp.where(tril, 1.0, 0.0).astype(BF16)
    ones_b = jnp.where(same_chunk, 1.0, 0.0).astype(BF16)
    lane_lo = c < CHUNK
    row_lo = r < CHUNK
    dot = functools.partial(jnp.dot, preferred_element_type=F32)

    for b0 in range(0, rows, blk):
        nchunk = min(blk, rows - b0) // CHUNK
        for hp in range(pairs):
            cols = slice(hp * LANES, (hp + 1) * LANES)
            qb = q_ref[b0:b0 + blk, cols] if nchunk == 2 else _pad_rows(q_ref[b0:b0 + CHUNK, cols], blk)
            kb = k_ref[b0:b0 + blk, cols] if nchunk == 2 else _pad_rows(k_ref[b0:b0 + CHUNK, cols], blk)
            gb = g_ref[b0:b0 + blk, cols] if nchunk == 2 else _pad_rows(g_ref[b0:b0 + CHUNK, cols], blk)
            bcum = _mask_dot(tril_b, gb)
            btot = _mask_dot(ones_b, gb)
            qe = qb * jnp.exp(bcum) * GLA_SCALE
            ke = kb * jnp.exp(-bcum)
            kt_t = (kb * jnp.exp(btot - bcum)).T
            btot_t = btot.T
            btot_sw = pltpu.roll(btot_t, CHUNK, axis=1)
            decay = (jnp.exp(jnp.where(lane_lo, btot_t, btot_sw)),
                     jnp.exp(jnp.where(lane_lo, btot_sw, btot_t)))
            keb = ke.astype(BF16)
            vbs, qms, intra = [], [], []
            for hh in range(2):
                h = 2 * hp + hh
                vcols = slice(h * LANES, (h + 1) * LANES)
                vb = v_ref[b0:b0 + blk, vcols] if nchunk == 2 else _pad_rows(v_ref[b0:b0 + CHUNK, vcols], blk)
                vb = vb.astype(BF16)
                qm = jnp.where(lane_lo if hh == 0 else jnp.logical_not(lane_lo), qe, 0.0).astype(BF16)
                a = jnp.where(tril, _dot_nt(qm, keb), 0.0)
                intra.append(dot(a.astype(BF16), vb))
                vbs.append(vb)
                qms.append(qm)
            state = s_ref[hp]
            outs = [[], []]
            for ch in range(nchunk):
                sb = state.astype(BF16)
                rs = slice(ch * CHUNK, (ch + 1) * CHUNK)
                for hh in range(2):
                    outs[hh].append(dot(qms[hh][rs], sb) + intra[hh][rs])
                kt_c = jnp.where(lane_lo if ch == 0 else jnp.logical_not(lane_lo), kt_t, 0.0).astype(BF16)
                upd = jnp.where(row_lo, dot(kt_c, vbs[0]), dot(kt_c, vbs[1]))
                state = decay[ch] * state + upd
            s_ref[hp] = state
            for hh in range(2):
                h = 2 * hp + hh
                res = outs[hh][0] if nchunk == 1 else jnp.concatenate(outs[hh], axis=0)
                o_ref[b0:b0 + nchunk * CHUNK, h * LANES:(h + 1) * LANES] = res

    @pl.when(t == pl.num_programs(1) - 1)
    def _():
        sf_ref[0] = s_ref[...]


def _pad_rows(a, rows):
    return jnp.concatenate([a, jnp.zeros((rows - a.shape[0], a.shape[1]), a.dtype)], axis=0)


def _gla(q, k, g, v, s0, batch, seq):
    tg = _row_tile(seq, GLA_ROWS)
    nt = seq // tg
    pairs = GLA_HEADS // 2
    dk = GLA_HEADS * GLA_DK
    row = lambda width: pl.BlockSpec((tg, width), lambda b, t: (b * nt + t, 0))
    state = pl.BlockSpec((1, pairs, LANES, LANES), lambda b, t: (b, 0, 0, 0))
    return pl.pallas_call(
        _gla_kernel,
        out_shape=(jax.ShapeDtypeStruct((batch * seq, GLA_W), F32),
                   jax.ShapeDtypeStruct((batch, pairs, LANES, LANES), F32)),
        grid=(batch, nt),
        in_specs=[row(dk), row(dk), row(dk), row(GLA_W), state],
        out_specs=(row(GLA_W), state),
        scratch_shapes=[pltpu.VMEM((pairs, LANES, LANES), F32)],
        compiler_params=_cparams(("arbitrary", "arbitrary")),
        name="gla_recurrence",
    )(q, k, g, v, s0)


def _merge_kernel(x_ref, of_ref, fg_ref, om_ref, mg_ref, og_ref, gg_ref, gn_ref, w_ref, lg_ref, lb_ref,
                  o_ref, y_ref, *, alpha):
    for h in range(FOX_HEADS):
        cols = slice(h * LANES, (h + 1) * LANES)
        y_ref[:, cols] = (of_ref[h].astype(F32) * _silu(fg_ref[:, cols].astype(F32))).astype(BF16)
    for h in range(MLA_HEADS):
        cols = slice(h * LANES, (h + 1) * LANES)
        y_ref[:, FOX_W + h * LANES:FOX_W + (h + 1) * LANES] = (
            om_ref[h].astype(F32) * _silu(mg_ref[:, cols].astype(F32))).astype(BF16)
    for h in range(GLA_HEADS):
        cols = slice(h * GLA_DV, (h + 1) * GLA_DV)
        og = og_ref[:, cols]
        og = og * lax.rsqrt(jnp.mean(og * og, axis=-1, keepdims=True) + RMS_EPS) * gn_ref[...]
        base = FOX_W + MLA_W + h * GLA_DV
        y_ref[:, base:base + GLA_DV] = (og * _silu(gg_ref[:, cols].astype(F32))).astype(BF16)
    z = alpha * x_ref[...] + jnp.dot(y_ref[...], w_ref[...], preferred_element_type=F32)
    mu = jnp.mean(z, axis=-1, keepdims=True)
    zc = z - mu
    var = jnp.mean(zc * zc, axis=-1, keepdims=True)
    o_ref[...] = zc * lax.rsqrt(var + LN_EPS) * lg_ref[...] + lb_ref[...]


def _merge(x2, o_fox, fg, o_mla, mg, o_gla, gg, gnorm, w_out, ln_g, ln_b, alpha):
    rows, d = x2.shape
    tm = _row_tile(rows, MERGE_ROWS)
    row = lambda width: pl.BlockSpec((tm, width), lambda i: (i, 0))
    head = lambda n: pl.BlockSpec((n, tm, LANES), lambda i: (0, i, 0))
    const = lambda a: pl.BlockSpec(a.shape, lambda i: (0, 0))
    return pl.pallas_call(
        functools.partial(_merge_kernel, alpha=alpha),
        out_shape=jax.ShapeDtypeStruct((rows, d), F32),
        grid=(rows // tm,),
        in_specs=[row(d), head(FOX_HEADS), row(FOX_W), head(MLA_HEADS), row(MLA_W), row(GLA_W), row(GLA_W),
                  const(gnorm), const(w_out), const(ln_g), const(ln_b)],
        out_specs=row(d),
        scratch_shapes=[pltpu.VMEM((tm, d), BF16)],
        compiler_params=_cparams(("arbitrary",)),
        name="merge",
    )(x2, o_fox, fg, o_mla, mg, o_gla, gg, gnorm, w_out, ln_g, ln_b)


def _past_cumsum_kernel(lf_ref, o_ref):
    past = lf_ref.shape[2]
    tl = min(past, 512)
    r = _iota((tl, tl), 0)
    c = _iota((tl, tl), 1)
    triu = jnp.where(r <= c, 1.0, 0.0).astype(BF16)
    d = functools.partial(jnp.dot, preferred_element_type=F32)
    carry = jnp.zeros((SUBLANES, 1), F32)
    for j in range(0, past, tl):
        a1, a2, a3 = _split3(lf_ref[0, :, j:j + tl])
        cum = d(a1, triu) + d(a2, triu) + d(a3, triu) + carry
        o_ref[0, :, j:j + tl] = cum
        carry = cum[:, tl - 1:tl]
    o_ref[0] = (o_ref[0] - carry) * LOG2E


def _past_cumsum(lf_rows):
    n, _, past = lf_rows.shape
    spec = pl.BlockSpec((1, SUBLANES, past), lambda i: (i, 0, 0))
    return pl.pallas_call(
        _past_cumsum_kernel,
        out_shape=jax.ShapeDtypeStruct(lf_rows.shape, F32),
        grid=(n,),
        in_specs=[spec],
        out_specs=spec,
        compiler_params=_cparams(("arbitrary",)),
        name="past_cumsum",
    )(lf_rows)


def _sample_softmax_step(s, v, m_ref, l_ref, acc_ref, h):
    m_prev = m_ref[h]
    m_next = jnp.maximum(m_prev, jnp.max(s, axis=1, keepdims=True))
    alpha = jnp.exp2(m_prev - m_next)
    p = jnp.exp2(s - m_next[:, :1])
    l_ref[h] = alpha * l_ref[h] + jnp.sum(p, axis=1, keepdims=True)
    acc_ref[h] = alpha * acc_ref[h] + jnp.dot(p.astype(BF16), v, preferred_element_type=F32)
    m_ref[h] = m_next


def _fox_sample_kernel(q_ref, ck_ref, cv_ref, cp_ref, kn_ref, vn_ref, cc_ref, cr_ref,
                       o_ref, m_ref, l_ref, acc_ref):
    j = pl.program_id(1)
    s_new = q_ref.shape[1]

    @pl.when(j == 0)
    def _():
        m_ref[...] = jnp.full_like(m_ref, -jnp.inf)
        l_ref[...] = jnp.zeros_like(l_ref)
        acc_ref[...] = jnp.zeros_like(acc_ref)

    for h in range(FOX_HEADS):
        s = _dot_nt(q_ref[h, :, :LANES], ck_ref[0, 0, h].astype(BF16))
        s = s + (cc_ref[:, h:h + 1] - cp_ref[0, h:h + 1, :])
        _sample_softmax_step(s, cv_ref[0, 0, h].astype(BF16), m_ref, l_ref, acc_ref, h)

    @pl.when(j == pl.num_programs(1) - 1)
    def _():
        keep = _iota((s_new, s_new), 1) <= _iota((s_new, s_new), 0)
        for h in range(FOX_HEADS):
            s = _dot_nt(q_ref[h, :, :LANES], kn_ref[h, :, :LANES])
            s = s + (cc_ref[:, h:h + 1] - cr_ref[0, h:h + 1, :])
            s = jnp.where(keep, s, -jnp.inf)
            _sample_softmax_step(s, vn_ref[h], m_ref, l_ref, acc_ref, h)
            o_ref[h] = (acc_ref[h] / l_ref[h]).astype(BF16)


def _fox_sample(q, cache_k, cache_v, c_past, k_new, v_new, ccol, crow, layer):
    _, batch, _, past, _ = cache_k.shape
    s_new = q.shape[1] // batch
    tk = _row_tile(past, SAMPLE_TK)
    new = lambda width: pl.BlockSpec((FOX_HEADS, s_new, width), lambda b, j: (0, b, 0))
    cache = pl.BlockSpec((1, 1, FOX_HEADS, tk, FOX_HD), lambda b, j: (layer, b, 0, j, 0))
    return pl.pallas_call(
        _fox_sample_kernel,
        out_shape=jax.ShapeDtypeStruct((FOX_HEADS, batch * s_new, FOX_HD), BF16),
        grid=(batch, past // tk),
        in_specs=[new(QK), cache, cache,
                  pl.BlockSpec((1, SUBLANES, tk), lambda b, j: (layer * batch + b, 0, j)),
                  new(QK), new(FOX_HD),
                  pl.BlockSpec((s_new, LANES), lambda b, j: (b, 0)),
                  pl.BlockSpec((1, SUBLANES, s_new), lambda b, j: (b, 0, 0))],
        out_specs=new(FOX_HD),
        scratch_shapes=[pltpu.VMEM((FOX_HEADS, s_new, LANES), F32),
                        pltpu.VMEM((FOX_HEADS, s_new, LANES), F32),
                        pltpu.VMEM((FOX_HEADS, s_new, FOX_HD), F32)],
        compiler_params=_cparams(("arbitrary", "arbitrary")),
        name="fox_sample_attn",
    )(q, cache_k, cache_v, c_past, k_new, v_new, ccol, crow)


def _mla_sample_kernel(q_ref, wuk_ref, wuv_ref, cc_ref, cr_ref, cn_ref, rn_ref,
                       o_ref, ql_ref, qr_ref, m_ref, l_ref, acc_ref):
    j = pl.program_id(1)
    s_new = q_ref.shape[1]
    dot = functools.partial(jnp.dot, preferred_element_type=F32)

    @pl.when(j == 0)
    def _():
        m_ref[...] = jnp.full_like(m_ref, -jnp.inf)
        l_ref[...] = jnp.zeros_like(l_ref)
        acc_ref[...] = jnp.zeros_like(acc_ref)
        for h in range(MLA_HEADS):
            rs = slice(h * s_new, (h + 1) * s_new)
            ql_ref[rs, :] = dot(q_ref[h, :, :LANES], wuk_ref[h]).astype(BF16)
            qr_ref[rs, :] = q_ref[h, :, LANES:]

    def update(c_keys, r_keys):
        s = _dot_nt(ql_ref[...], c_keys) + _dot_nt(qr_ref[...], r_keys)
        m_prev = m_ref[...]
        m_next = jnp.maximum(m_prev, jnp.max(s, axis=1, keepdims=True))
        alpha = jnp.exp2(m_prev - m_next)
        p = jnp.exp2(s - m_next[:, :1])
        l_ref[...] = alpha * l_ref[...] + jnp.sum(p, axis=1, keepdims=True)
        acc_ref[...] = alpha[:, :1] * acc_ref[...] + dot(p.astype(BF16), c_keys)
        m_ref[...] = m_next

    update(cc_ref[0, 0].astype(BF16), cr_ref[0, 0].astype(BF16))

    @pl.when(j == pl.num_programs(1) - 1)
    def _():
        rn = jnp.where(_iota((s_new, LANES), 1) < MLA_ROPE, rn_ref[...], 0.0)
        update(cn_ref[...].astype(BF16), rn.astype(BF16))
        for h in range(MLA_HEADS):
            rs = slice(h * s_new, (h + 1) * s_new)
            o_lat = (acc_ref[rs, :] / l_ref[rs, :1]).astype(BF16)
            o_ref[h] = dot(o_lat, wuv_ref[h]).astype(BF16)


def _mla_sample(q, wuk, wuv, cache_c, cache_r, c_new, r_new, layer):
    _, batch, past, _ = cache_c.shape
    s_new = q.shape[1] // batch
    tk = _row_tile(past, SAMPLE_TK)
    rows = MLA_HEADS * s_new
    new = lambda width: pl.BlockSpec((s_new, width), lambda b, j: (b, 0))
    head = lambda width: pl.BlockSpec((MLA_HEADS, s_new, width), lambda b, j: (0, b, 0))
    const = lambda a: pl.BlockSpec(a.shape, lambda b, j: (0, 0, 0))
    return pl.pallas_call(
        _mla_sample_kernel,
        out_shape=jax.ShapeDtypeStruct((MLA_HEADS, batch * s_new, MLA_V), BF16),
        grid=(batch, past // tk),
        in_specs=[head(QK), const(wuk), const(wuv),
                  pl.BlockSpec((1, 1, tk, MLA_KV_RANK), lambda b, j: (layer, b, j, 0)),
                  pl.BlockSpec((1, 1, tk, LANES), lambda b, j: (layer, b, j, 0)),
                  new(MLA_KV_RANK), new(LANES)],
        out_specs=head(MLA_V),
        scratch_shapes=[pltpu.VMEM((rows, MLA_KV_RANK), BF16),
                        pltpu.VMEM((rows, LANES), BF16),
                        pltpu.VMEM((rows, LANES), F32),
                        pltpu.VMEM((rows, LANES), F32),
                        pltpu.VMEM((rows, MLA_KV_RANK), F32)],
        compiler_params=_cparams(("arbitrary", "arbitrary")),
        name="mla_sample_attn",
    )(q, wuk, wuv, cache_c, cache_r, c_new, r_new)


def _pad_cols(a, width):
    return jnp.pad(a, ((0, 0), (0, width - a.shape[1])))


W_FOX_COLS = 4 * FOX_W + LANES
W_MLA_COLS = MLA_HEADS * QK + MLA_KV_RANK + LANES + MLA_W
W_GLA_COLS = 2 * GLA_HEADS * GLA_DK + GLA_W + LANES + GLA_W
W_IN_BLOCK_COLS = -(-IN_OFFS[-1] // LANES) * LANES
PACK_ROWS = 256


def _pack_kernel(w_ref, fox_ref, mla_ref, gla_ref):
    rows = w_ref.shape[1]
    lane = _iota((rows, LANES), 1)

    def take(src, width):
        start = (src // LANES) * LANES
        shift = src - start
        span = -(-(shift + width) // LANES) * LANES
        win = w_ref[0, :, start:start + span]
        if shift:
            win = pltpu.roll(win, span - shift, axis=1)
        return win[:, :width]

    def narrow(src, width):
        return jnp.where(lane < width, take(src, LANES), 0.0)

    def rope_pair(src):
        p = take(src, LANES)
        quarter = MLA_ROPE // 2
        return jnp.where(lane < MLA_ROPE, p,
                         jnp.where(lane < MLA_ROPE + quarter, pltpu.roll(p, quarter, axis=1),
                                   pltpu.roll(p, LANES - quarter, axis=1)))

    def put(ref, col, val):
        ref[0, :, col:col + val.shape[1]] = val.astype(BF16)

    fq, fk, fv, ff, fg, mq, mc, mr, mg, gq, gk, gv, ggk, gg = IN_OFFS[:-1]
    put(fox_ref, 0, take(fq, FOX_W))
    put(fox_ref, FOX_W, take(fk, FOX_W))
    put(fox_ref, 2 * FOX_W, take(fv, FOX_W))
    put(fox_ref, 3 * FOX_W, take(fg, FOX_W))
    put(fox_ref, 4 * FOX_W, narrow(ff, FOX_HEADS))
    for h in range(MLA_HEADS):
        src = mq + h * (MLA_NOPE + MLA_ROPE)
        put(mla_ref, h * QK, take(src, MLA_NOPE))
        put(mla_ref, h * QK + LANES, rope_pair(src + MLA_NOPE))
    col = MLA_HEADS * QK
    put(mla_ref, col, take(mc, MLA_KV_RANK))
    put(mla_ref, col + MLA_KV_RANK, rope_pair(mr))
    put(mla_ref, col + MLA_KV_RANK + LANES, take(mg, MLA_W))
    dk = GLA_HEADS * GLA_DK
    put(gla_ref, 0, take(gq, dk))
    put(gla_ref, dk, take(gk, dk))
    put(gla_ref, 2 * dk, take(gv, GLA_W))
    put(gla_ref, 2 * dk + GLA_W, narrow(ggk, GLA_GATE_RANK))
    put(gla_ref, 2 * dk + GLA_W + LANES, take(gg, GLA_W))


def _pack_w_in(w_in):
    depth, d, _ = w_in.shape
    tr = _row_tile(d, PACK_ROWS)
    out = lambda cols: pl.BlockSpec((1, tr, cols), lambda l, i: (l, i, 0))
    return pl.pallas_call(
        _pack_kernel,
        out_shape=(jax.ShapeDtypeStruct((depth, d, W_FOX_COLS), BF16),
                   jax.ShapeDtypeStruct((depth, d, W_MLA_COLS), BF16),
                   jax.ShapeDtypeStruct((depth, d, W_GLA_COLS), BF16)),
        grid=(depth, d // tr),
        in_specs=[pl.BlockSpec((1, tr, W_IN_BLOCK_COLS), lambda l, i: (l, i, 0))],
        out_specs=(out(W_FOX_COLS), out(W_MLA_COLS), out(W_GLA_COLS)),
        compiler_params=_cparams(("arbitrary", "arbitrary")),
        name="pack_w_in",
    )(w_in)


def _layer_params(w_fox, w_mla, w_gla, b_f, kv_norm, w_up, w_gk_up, b_gk, gla_norm, w_out, ln_g, ln_b):
    w_up_k = w_up[:, :, :MLA_NOPE]
    w_up_v = w_up[:, :, MLA_NOPE:]
    return dict(
        w_fox=w_fox,
        b_f=_pad_cols(b_f[None, :], LANES),
        w_mla=w_mla,
        kv_norm=kv_norm[None, :],
        w_up=jnp.concatenate([w_up_k.reshape(MLA_KV_RANK, MLA_W),
                              w_up_v.reshape(MLA_KV_RANK, MLA_W)], axis=1).astype(BF16),
        w_uk=jnp.transpose(w_up_k, (1, 2, 0)).astype(BF16),
        w_uv=jnp.transpose(w_up_v, (1, 0, 2)).astype(BF16),
        w_gla=w_gla,
        w_gk=jnp.pad(w_gk_up, ((0, LANES - GLA_GATE_RANK), (0, 0))).astype(BF16),
        b_gk=b_gk[None, :],
        gla_norm=gla_norm[None, :],
        w_out=w_out.astype(BF16),
        ln_g=ln_g[None, :],
        ln_b=ln_b[None, :],
    )


def _rope_table(pos, rows):
    half = MLA_ROPE // 2
    inv = ROPE_BASE ** (-jnp.arange(half, dtype=F32) / half)
    ang = pos.astype(F32)[:, None] * inv[None, :]
    cos, sin = jnp.cos(ang), jnp.sin(ang)
    tab = jnp.concatenate([cos, cos, -sin, sin], axis=1)
    reps = max(rows // tab.shape[0], 1)
    return jnp.tile(tab, (reps, 1))


def _branches(x2, p, sel, tab, batch, seq, with_kv):
    fox = _fox_proj(x2, p["w_fox"], p["b_f"], sel, batch, seq)
    mla = _mla_proj(x2, p["w_mla"], tab, p["kv_norm"], p["w_up"], with_kv)
    gla = _gla_proj(x2, p["w_gla"], p["w_gk"], p["b_gk"])
    return fox, mla, gla


def kernel(x_prompt, x_sample, cache_fox_k, cache_fox_v, cache_fox_logf, cache_mla_ckv, cache_mla_krope,
           state_gla, w_in, b_fox_f, mla_kv_norm, w_mla_kv_up, w_gla_gk_up, b_gla_gk, gla_norm, w_out,
           ln_g, ln_b):
    depth = w_in.shape[0]
    bp, tp, d = x_prompt.shape
    bs, ts, _ = x_sample.shape
    past = cache_fox_k.shape[2]
    alpha = (2 * depth) ** 0.25
    pairs = GLA_HEADS // 2

    tab_p = _rope_table(jnp.arange(tp), _row_tile(bp * tp, PROJ_ROWS))
    tab_s = _rope_table(past + jnp.arange(ts), _row_tile(bs * ts, PROJ_ROWS))
    sel = _fox_bias_selectors()

    cache_k = jnp.transpose(cache_fox_k, (0, 1, 3, 2, 4))
    cache_v = jnp.transpose(cache_fox_v, (0, 1, 3, 2, 4))
    lf_rows = jnp.pad(jnp.swapaxes(cache_fox_logf.astype(F32), 2, 3).reshape(depth * bs, FOX_HEADS, past),
                      ((0, 0), (0, SUBLANES - FOX_HEADS), (0, 0)))
    c_past = _past_cumsum(lf_rows)
    cache_r = jnp.pad(cache_mla_krope, ((0, 0), (0, 0), (0, 0), (0, LANES - MLA_ROPE)))

    xp = x_prompt.reshape(bp * tp, d)
    xs = x_sample.reshape(bs * ts, d)
    zero_state = jnp.zeros((bp, pairs, LANES, LANES), F32)
    outs = [[] for _ in range(12)]

    w_fox, w_mla, w_gla = _pack_w_in(w_in)

    for l in range(depth):
        p = _layer_params(w_fox[l], w_mla[l], w_gla[l], b_fox_f[l], mla_kv_norm[l], w_mla_kv_up[l], w_gla_gk_up[l],
                          b_gla_gk[l], gla_norm[l], w_out[l], ln_g[l], ln_b[l])

        (fq, fkc, fvb, fk, fv, fg, flf, _, _), (mq, mc, mkr, _, mg, mkc, mv), (gq, gk, gv, ggk, gg) = \
            _branches(xp, p, sel, tab_p, bp, tp, True)
        o_f = _prompt_attn(fq, fkc, fvb, bp, tp, chunked=False)
        o_m = _prompt_attn(mq, mkc, mv, bp, tp, chunked=True)
        o_g, s_p = _gla(gq, gk, ggk, gv, zero_state, bp, tp)
        outs[0].append(fk)
        outs[1].append(fv)
        outs[2].append(flf[:, :FOX_HEADS].reshape(bp, tp, FOX_HEADS))
        outs[3].append(mc.reshape(bp, tp, MLA_KV_RANK))
        outs[4].append(mkr.reshape(bp, tp, MLA_ROPE))
        outs[5].append(s_p.reshape(bp, GLA_HEADS, GLA_DK, GLA_DV))
        xp = _merge(xp, o_f, fg, o_m, mg, o_g, gg, p["gla_norm"], p["w_out"], p["ln_g"], p["ln_b"], alpha)

        (fq, fkc, fvb, fk, fv, fg, flf, fcc, fcr), (mq, mc, mkr, mkr2, mg), (gq, gk, gv, ggk, gg) = \
            _branches(xs, p, sel, tab_s, bs, ts, False)
        crow_s = jnp.swapaxes(fcr.reshape(SUBLANES, bs, ts), 0, 1)
        o_f = _fox_sample(fq, cache_k, cache_v, c_past, fkc, fvb, fcc, crow_s, l)
        o_m = _mla_sample(mq, p["w_uk"], p["w_uv"], cache_mla_ckv, cache_r, mc, mkr2, l)
        o_g, s_s = _gla(gq, gk, ggk, gv, state_gla[l].reshape(bs, pairs, LANES, LANES), bs, ts)
        outs[6].append(fk)
        outs[7].append(fv)
        outs[8].append(flf[:, :FOX_HEADS].reshape(bs, ts, FOX_HEADS))
        outs[9].append(mc.reshape(bs, ts, MLA_KV_RANK))
        outs[10].append(mkr.reshape(bs, ts, MLA_ROPE))
        outs[11].append(s_s.reshape(bs, GLA_HEADS, GLA_DK, GLA_DV))
        xs = _merge(xs, o_f, fg, o_m, mg, o_g, gg, p["gla_norm"], p["w_out"], p["ln_g"], p["ln_b"], alpha)

    stacked = [jnp.stack(o) for o in outs]
    for i in (0, 1, 6, 7):
        stacked[i] = jnp.transpose(stacked[i], (0, 1, 3, 2, 4))
    return (xp.reshape(bp, tp, d), xs.reshape(bs, ts, d)) + tuple(stacked)
```

```python
import functools
import math

import numpy as np
import jax
import jax.numpy as jnp
from jax import lax
from jax.experimental import pallas as pl
from jax.experimental.pallas import tpu as pltpu

F32 = jnp.float32
BF16 = jnp.bfloat16

LANES = 128
SUBLANES = 8
VMEM_LIMIT_BYTES = 60 * 1024 * 1024

CHUNK = 64
FOX_HEADS, FOX_HD = 6, 128
MLA_HEADS, MLA_NOPE, MLA_ROPE, MLA_V, MLA_KV_RANK = 6, 128, 64, 128, 512
GLA_HEADS, GLA_DK, GLA_DV, GLA_GATE_RANK = 4, 64, 128, 16
GLA_GATE_NORM = 16.0
FOX_W = FOX_HEADS * FOX_HD
MLA_W = MLA_HEADS * MLA_V
GLA_W = GLA_HEADS * GLA_DV
QK = 2 * LANES
ROPE_BASE = 10000.0
LN_EPS = 1e-5
RMS_EPS = 1e-6
LOG2E = math.log2(math.e)
FOX_SCALE = FOX_HD ** -0.5
MLA_SCALE = (MLA_NOPE + MLA_ROPE) ** -0.5
GLA_SCALE = GLA_DK ** -0.5

IN_SIZES = (FOX_W, FOX_W, FOX_W, FOX_HEADS, FOX_W,
            MLA_HEADS * (MLA_NOPE + MLA_ROPE), MLA_KV_RANK, MLA_ROPE, MLA_W,
            GLA_HEADS * GLA_DK, GLA_HEADS * GLA_DK, GLA_W, GLA_GATE_RANK, GLA_W)
IN_OFFS = tuple(int(v) for v in np.concatenate([[0], np.cumsum(IN_SIZES)]))

PROJ_ROWS = 512
MERGE_ROWS = 512
ATTN_TQ = 1024
ATTN_TK = 1024
GLA_ROWS = 256
SAMPLE_TK = 2048


def _cparams(sem):
    return pltpu.CompilerParams(dimension_semantics=sem, vmem_limit_bytes=VMEM_LIMIT_BYTES)


def _row_tile(rows, want):
    t = min(rows, want)
    assert rows % t == 0, (rows, t)
    return t


def _split3(a):
    a1 = a.astype(BF16)
    r1 = a - a1.astype(F32)
    a2 = r1.astype(BF16)
    a3 = (r1 - a2.astype(F32)).astype(BF16)
    return a1, a2, a3


def _mask_dot(mask_bf16, a):
    a1, a2, a3 = _split3(a)
    d = functools.partial(jnp.dot, preferred_element_type=F32)
    return d(mask_bf16, a1) + d(mask_bf16, a2) + d(mask_bf16, a3)


def _dot_nt(a, b):
    return lax.dot_general(a, b, (((1,), (1,)), ((), ())), preferred_element_type=F32)


def _log_sigmoid(z):
    return jnp.minimum(z, 0.0) - jnp.log1p(jnp.exp(-jnp.abs(z)))


def _silu(z):
    return z / (1.0 + jnp.exp(-z))


def _iota(shape, dim):
    return lax.broadcasted_iota(jnp.int32, shape, dim)


def _div_pow2(x, n):
    assert n > 0 and n & (n - 1) == 0, n
    return lax.shift_right_logical(x, jnp.int32(n.bit_length() - 1))


def _layer_spec(a, layer):
    zeros = (0,) * (a.ndim - 1)
    return pl.BlockSpec((None,) + a.shape[1:], lambda *_: (layer,) + zeros)


def _stacked_outputs(prev, n_inputs, first_out):
    if prev is None:
        return [], [], {}
    specs = [pl.BlockSpec(memory_space=pl.ANY)] * len(prev)
    return list(prev), specs, {n_inputs + j: first_out + j for j in range(len(prev))}


def _lane_tile(a, width):
    return jnp.concatenate([a] * (width // LANES), axis=1)


def _fox_bias_selectors():
    sq = np.zeros((3 * LANES, FOX_W), np.float32)
    sk = np.zeros((3 * LANES, FOX_W), np.float32)
    oq = np.zeros((1, FOX_W), np.float32)
    ok = np.zeros((1, FOX_W), np.float32)
    for h in range(FOX_HEADS):
        for piece in range(3):
            sq[piece * LANES + h, h * LANES + piece] = 1.0
            sk[piece * LANES + h, h * LANES + 3 + piece] = -1.0
            oq[0, h * LANES + 3 + piece] = 1.0
            ok[0, h * LANES + piece] = 1.0
    return jnp.asarray(sq, BF16), jnp.asarray(sk, BF16), jnp.asarray(oq), jnp.asarray(ok)


def _fox_proj_kernel(x_ref, w_ref, bf_ref, sq_ref, sk_ref, oq_ref, ok_ref, *refs, seq, tiles_per_seq):
    kf_ref, vf_ref, lf_ref, q_ref, k_ref, v_ref, g_ref, cc_ref, cr_ref, carry_ref = refs[-10:]
    tm = x_ref.shape[0]
    nb, _, tt, _ = kf_ref.shape
    xb = x_ref[...].astype(BF16)
    dot = functools.partial(jnp.dot, preferred_element_type=F32)

    def proj(a, b):
        return dot(xb, w_ref[:, a:b])

    q = proj(0, FOX_W) * (FOX_SCALE * LOG2E)
    k = proj(FOX_W, 2 * FOX_W)
    v = proj(2 * FOX_W, 3 * FOX_W)
    g_ref[...] = proj(3 * FOX_W, 4 * FOX_W).astype(BF16)
    lf = _log_sigmoid(proj(4 * FOX_W, 4 * FOX_W + LANES) + bf_ref[...])
    lf_ref[...] = lf

    r = _iota((tm, tm), 0)
    c = _iota((tm, tm), 1)
    tri = r >= c
    if seq < tm:
        tri = jnp.logical_and(tri, _div_pow2(r, seq) == _div_pow2(c, seq))
    c_tile = _mask_dot(jnp.where(tri, 1.0, 0.0).astype(BF16), lf)
    if seq > tm:
        @pl.when(pl.program_id(0) % tiles_per_seq == 0)
        def _():
            carry_ref[...] = jnp.zeros_like(carry_ref)
        c_tile = c_tile + carry_ref[...]
        carry_ref[...] = c_tile[tm - 1:tm, :]
    c2 = c_tile * LOG2E
    cc_ref[...] = c2
    cr_ref[...] = c2.T[:SUBLANES, :]

    pieces = jnp.concatenate(_split3(c2), axis=1)
    qx = dot(pieces, sq_ref[...]) + oq_ref[...]
    kx = dot(pieces, sk_ref[...]) + ok_ref[...]
    for h in range(FOX_HEADS):
        cols = slice(h * LANES, (h + 1) * LANES)
        q_ref[h, :, :LANES] = q[:, cols].astype(BF16)
        q_ref[h, :, LANES:] = qx[:, cols].astype(BF16)
        k_ref[h, :, :LANES] = k[:, cols].astype(BF16)
        k_ref[h, :, LANES:] = kx[:, cols].astype(BF16)
        v_ref[h] = v[:, cols].astype(BF16)
        for b in range(nb):
            kf_ref[b, h] = k[b * tt:(b + 1) * tt, cols]
            vf_ref[b, h] = v[b * tt:(b + 1) * tt, cols]


def _fox_proj(x2, w, bf, sel, batch, seq, layer, prev):
    depth = w.shape[0]
    rows, d = x2.shape
    tm = _row_tile(rows, PROJ_ROWS)
    assert seq % tm == 0 or tm % seq == 0
    tiles_per_seq = max(seq // tm, 1)
    nb, tt = max(tm // seq, 1), min(tm, seq)
    row = lambda width: pl.BlockSpec((tm, width), lambda i: (i, 0))
    head = lambda width: pl.BlockSpec((FOX_HEADS, tm, width), lambda i: (0, i, 0))
    const = lambda a: pl.BlockSpec(a.shape, lambda i: (0, 0))
    cache = pl.BlockSpec((None, nb, FOX_HEADS, tt, FOX_HD),
                         lambda i: (layer, i // tiles_per_seq, 0, i % tiles_per_seq, 0))
    out_shape = (
        jax.ShapeDtypeStruct((depth, batch, FOX_HEADS, seq, FOX_HD), F32),
        jax.ShapeDtypeStruct((depth, batch, FOX_HEADS, seq, FOX_HD), F32),
        jax.ShapeDtypeStruct((depth, rows, LANES), F32),
        jax.ShapeDtypeStruct((FOX_HEADS, rows, QK), BF16),
        jax.ShapeDtypeStruct((FOX_HEADS, rows, QK), BF16),
        jax.ShapeDtypeStruct((FOX_HEADS, rows, FOX_HD), BF16),
        jax.ShapeDtypeStruct((rows, FOX_W), BF16),
        jax.ShapeDtypeStruct((rows, LANES), F32),
        jax.ShapeDtypeStruct((SUBLANES, rows), F32),
    )
    out_specs = (cache, cache, pl.BlockSpec((None, tm, LANES), lambda i: (layer, i, 0)),
                 head(QK), head(QK), head(FOX_HD), row(FOX_W), row(LANES),
                 pl.BlockSpec((SUBLANES, tm), lambda i: (0, i)))
    in_specs = [row(d), _layer_spec(w, layer), _layer_spec(bf, layer)] + [const(a) for a in sel]
    prev_args, prev_specs, aliases = _stacked_outputs(prev, len(in_specs), 0)
    return pl.pallas_call(
        functools.partial(_fox_proj_kernel, seq=seq, tiles_per_seq=tiles_per_seq),
        out_shape=out_shape,
        grid=(rows // tm,),
        in_specs=in_specs + prev_specs,
        out_specs=out_specs,
        scratch_shapes=[pltpu.VMEM((1, LANES), F32)],
        input_output_aliases=aliases,
        compiler_params=_cparams(("arbitrary",)),
        name="fox_proj",
    )(x2, w, bf, *sel, *prev_args)


def _rope_block(blk, tab):
    t = blk * tab
    return t + pltpu.roll(t, LANES // 2, axis=1)


def _mla_proj_kernel(*refs, with_kv, n_prev):
    if with_kv:
        x_ref, w_ref, tab_ref, kvn_ref, wup_ref = refs[:5]
        c_ref, kr_ref, q_ref, kr2_ref, g_ref, kc_ref, v_ref = refs[5 + n_prev:]
    else:
        x_ref, w_ref, tab_ref, kvn_ref = refs[:4]
        c_ref, kr_ref, q_ref, kr2_ref, g_ref = refs[4 + n_prev:]
    tm = x_ref.shape[0]
    xb = x_ref[...].astype(BF16)
    tab = tab_ref[...]
    dot = functools.partial(jnp.dot, preferred_element_type=F32)

    def proj(a, b):
        return dot(xb, w_ref[:, a:b])

    for h in range(MLA_HEADS):
        hq = proj(h * QK, (h + 1) * QK) * (MLA_SCALE * LOG2E)
        q_ref[h, :, :LANES] = hq[:, :LANES].astype(BF16)
        q_ref[h, :, LANES:] = _rope_block(hq[:, LANES:], tab).astype(BF16)
    off = MLA_HEADS * QK
    mc = proj(off, off + MLA_KV_RANK)
    c = mc * lax.rsqrt(jnp.mean(mc * mc, axis=-1, keepdims=True) + RMS_EPS) * kvn_ref[...]
    c_ref[...] = c
    off += MLA_KV_RANK
    kr = _rope_block(proj(off, off + LANES), tab)
    kr_ref[...] = kr[:, :MLA_ROPE]
    kr2_ref[...] = kr
    off += LANES
    g_ref[...] = proj(off, off + MLA_W).astype(BF16)
    if with_kv:
        cb = c.astype(BF16)
        krz = jnp.where(_iota((tm, LANES), 1) < MLA_ROPE, kr, 0.0).astype(BF16)
        for h in range(MLA_HEADS):
            kc_ref[h, :, :LANES] = dot(cb, wup_ref[:, h * LANES:(h + 1) * LANES]).astype(BF16)
            kc_ref[h, :, LANES:] = krz
            v_ref[h] = dot(cb, wup_ref[:, MLA_W + h * LANES:MLA_W + (h + 1) * LANES]).astype(BF16)


def _mla_proj(x2, w, tab, kvn, wup, with_kv, layer, prev):
    depth = w.shape[0]
    rows, d = x2.shape
    tm = _row_tile(rows, PROJ_ROWS)
    tab_tiles = tab.shape[0] // tm
    row = lambda width: pl.BlockSpec((tm, width), lambda i: (i, 0))
    head = lambda width: pl.BlockSpec((MLA_HEADS, tm, width), lambda i: (0, i, 0))
    stacked = lambda width: pl.BlockSpec((None, tm, width), lambda i: (layer, i, 0))
    in_specs = [row(d), _layer_spec(w, layer),
                pl.BlockSpec((tm, LANES), lambda i: (i % tab_tiles, 0)),
                _layer_spec(kvn, layer)]
    args = [x2, w, tab, kvn]
    out_shape = [jax.ShapeDtypeStruct((depth, rows, MLA_KV_RANK), F32),
                 jax.ShapeDtypeStruct((depth, rows, MLA_ROPE), F32),
                 jax.ShapeDtypeStruct((MLA_HEADS, rows, QK), BF16),
                 jax.ShapeDtypeStruct((rows, LANES), F32),
                 jax.ShapeDtypeStruct((rows, MLA_W), BF16)]
    out_specs = [stacked(MLA_KV_RANK), stacked(MLA_ROPE), head(QK), row(LANES), row(MLA_W)]
    if with_kv:
        in_specs.append(_layer_spec(wup, layer))
        args.append(wup)
        out_shape += [jax.ShapeDtypeStruct((MLA_HEADS, rows, QK), BF16),
                      jax.ShapeDtypeStruct((MLA_HEADS, rows, MLA_V), BF16)]
        out_specs += [head(QK), head(MLA_V)]
    prev_args, prev_specs, aliases = _stacked_outputs(prev, len(in_specs), 0)
    return pl.pallas_call(
        functools.partial(_mla_proj_kernel, with_kv=with_kv, n_prev=len(prev_args)),
        out_shape=tuple(out_shape),
        grid=(rows // tm,),
        in_specs=in_specs + prev_specs,
        out_specs=tuple(out_specs),
        input_output_aliases=aliases,
        compiler_params=_cparams(("arbitrary",)),
        name="mla_proj_kv" if with_kv else "mla_proj",
    )(*args, *prev_args)


def _gla_proj_kernel(x_ref, w_ref, wgk_ref, bgk_ref, qe_ref, ke_ref, kt_ref, dec_ref, v_ref, g_ref):
    tm = x_ref.shape[0]
    xb = x_ref[...].astype(BF16)

    def proj(a, b):
        return jnp.dot(xb, w_ref[:, a:b], preferred_element_type=F32)

    dk = GLA_HEADS * GLA_DK
    q = proj(0, dk)
    k = proj(dk, 2 * dk)
    v_ref[...] = proj(2 * dk, 2 * dk + GLA_W).astype(BF16)
    off = 2 * dk + GLA_W
    low = proj(off, off + LANES).astype(BF16)
    z = jnp.dot(low, wgk_ref[...], preferred_element_type=F32) + bgk_ref[...]
    gate = _log_sigmoid(z) * (1.0 / GLA_GATE_NORM)
    off += LANES
    g_ref[...] = proj(off, off + GLA_W).astype(BF16)

    r = _iota((tm, tm), 0)
    c = _iota((tm, tm), 1)
    tril = jnp.logical_and(_div_pow2(r, CHUNK) == _div_pow2(c, CHUNK), r >= c)
    bcum = _mask_dot(jnp.where(tril, 1.0, 0.0).astype(BF16), gate)
    btot = jnp.concatenate(
        [jnp.broadcast_to(bcum[e - 1:e, :], (CHUNK, dk)) for e in range(CHUNK, tm + 1, CHUNK)], axis=0)
    qe_ref[...] = (q * jnp.exp(bcum) * GLA_SCALE).astype(BF16)
    ke_ref[...] = (k * jnp.exp(-bcum)).astype(BF16)
    kt_ref[...] = k * jnp.exp(btot - bcum)
    dec_ref[...] = jnp.exp(btot)


def _gla_proj(x2, w, wgk, bgk, layer):
    rows, d = x2.shape
    tm = _row_tile(rows, PROJ_ROWS)
    dk = GLA_HEADS * GLA_DK
    row = lambda width: pl.BlockSpec((tm, width), lambda i: (i, 0))
    const = lambda a: _layer_spec(a, layer)
    return pl.pallas_call(
        _gla_proj_kernel,
        out_shape=(jax.ShapeDtypeStruct((rows, dk), BF16),
                   jax.ShapeDtypeStruct((rows, dk), BF16),
                   jax.ShapeDtypeStruct((rows, dk), F32),
                   jax.ShapeDtypeStruct((rows, dk), F32),
                   jax.ShapeDtypeStruct((rows, GLA_W), BF16),
                   jax.ShapeDtypeStruct((rows, GLA_W), BF16)),
        grid=(rows // tm,),
        in_specs=[row(d), const(w), const(wgk), const(bgk)],
        out_specs=(row(dk), row(dk), row(dk), row(dk), row(GLA_W), row(GLA_W)),
        compiler_params=_cparams(("arbitrary",)),
        name="gla_proj",
    )(x2, w, wgk, bgk)


def _online_softmax_update(s, v, m_prev, acc_prev):
    tk = s.shape[1]
    m_next = jnp.maximum(m_prev, jnp.max(s, axis=1, keepdims=True))
    alpha = jnp.exp2(m_prev - m_next)
    p = jnp.exp2(s - _lane_tile(m_next, tk)).astype(BF16)
    v_ones = jnp.concatenate([v, jnp.ones((tk, LANES), BF16)], axis=1)
    acc = _lane_tile(alpha, QK) * acc_prev + jnp.dot(p, v_ones, preferred_element_type=F32)
    return m_next, acc


def _prompt_attn_kernel(qi_ref, ki_ref, first_ref, last_ref, mask_ref,
                        q_ref, k_ref, v_ref, o_ref, m_ref, acc_ref, bias_ref, s_ref, *, chunked):
    s_idx = pl.program_id(1)
    heads, tq, _ = q_ref.shape
    tk = k_ref.shape[1]
    half = tq // 2
    lo, hi = slice(0, half), slice(half, tq)

    @pl.when(first_ref[s_idx] == 1)
    def _():
        m_ref[...] = jnp.full_like(m_ref, -jnp.inf)
        acc_ref[...] = jnp.zeros_like(acc_ref)

    def sweep(masked):
        def scores(h, rows, slot):
            s = _dot_nt(q_ref[h, rows, :], k_ref[h])
            if masked:
                s = s + bias_ref[rows, :]
            s_ref[slot] = s

        def softmax_pv(h, rows, slot):
            m_next, acc = _online_softmax_update(s_ref[slot], v_ref[h], m_ref[h, rows, :], acc_ref[h, rows, :])
            m_ref[h, rows, :] = m_next
            acc_ref[h, rows, :] = acc

        scores(0, lo, 0)
        for h in range(heads):
            scores(h, hi, 1)
            softmax_pv(h, lo, 0)
            if h + 1 < heads:
                scores(h + 1, lo, 0)
            softmax_pv(h, hi, 1)

    @pl.when(mask_ref[s_idx] == 1)
    def _():
        qpos = qi_ref[s_idx] * tq + _iota((tq, tk), 0)
        kpos = ki_ref[s_idx] * tk + _iota((tq, tk), 1)
        if chunked:
            keep = _div_pow2(kpos, CHUNK) <= _div_pow2(qpos, CHUNK)
        else:
            keep = kpos <= qpos
        bias_ref[...] = jnp.where(keep, 0.0, -jnp.inf)
        sweep(True)

    @pl.when(mask_ref[s_idx] == 0)
    def _():
        sweep(False)

    @pl.when(last_ref[s_idx] == 1)
    def _():
        for h in range(heads):
            o_ref[h] = (acc_ref[h, :, :LANES] / acc_ref[h, :, LANES:]).astype(BF16)


def _attn_schedule(seq, tq, tk, chunked):
    qi, ki, first, last, mask = [], [], [], [], []
    for a in range(seq // tq):
        q_lo, q_hi = a * tq, (a + 1) * tq - 1
        if chunked:
            q_lo, q_hi = (q_lo // CHUNK) * CHUNK, (q_hi // CHUNK) * CHUNK + CHUNK - 1
        tiles = [b for b in range(seq // tk) if b * tk <= q_hi]
        for b in tiles:
            qi.append(a)
            ki.append(b)
            first.append(int(b == tiles[0]))
            last.append(int(b == tiles[-1]))
            mask.append(int((b + 1) * tk - 1 > q_lo))
    return [jnp.asarray(np.asarray(t, np.int32)) for t in (qi, ki, first, last, mask)]


def _prompt_attn(q, k, v, batch, seq, *, chunked):
    heads = q.shape[0]
    tq = _row_tile(seq, ATTN_TQ)
    tk = _row_tile(seq, ATTN_TK)
    nq, nk = seq // tq, seq // tk
    tabs = _attn_schedule(seq, tq, tk, chunked)
    steps = int(tabs[0].shape[0])
    qmap = lambda b, s, qi, ki, *_: (0, b * nq + qi[s], 0)
    kmap = lambda b, s, qi, ki, *_: (0, b * nk + ki[s], 0)
    return pl.pallas_call(
        functools.partial(_prompt_attn_kernel, chunked=chunked),
        out_shape=jax.ShapeDtypeStruct((heads, batch * seq, LANES), BF16),
        grid_spec=pltpu.PrefetchScalarGridSpec(
            num_scalar_prefetch=5,
            grid=(batch, steps),
            in_specs=[pl.BlockSpec((heads, tq, QK), qmap),
                      pl.BlockSpec((heads, tk, QK), kmap),
                      pl.BlockSpec((heads, tk, LANES), kmap)],
            out_specs=pl.BlockSpec((heads, tq, LANES), qmap),
            scratch_shapes=[pltpu.VMEM((heads, tq, LANES), F32),
                            pltpu.VMEM((heads, tq, QK), F32),
                            pltpu.VMEM((tq, tk), F32),
                            pltpu.VMEM((2, tq // 2, tk), F32)]),
        compiler_params=_cparams(("arbitrary", "arbitrary")),
        name="mla_attn" if chunked else "fox_attn",
    )(*tabs, q, k, v)


def _gla_kernel(qe_ref, ke_ref, kt_ref, dec_ref, v_ref, s0_ref, o_ref, sf_ref, s_ref):
    t = pl.program_id(1)
    rows = qe_ref.shape[0]
    blk = 2 * CHUNK
    pairs = GLA_HEADS // 2

    @pl.when(t == 0)
    def _():
        s_ref[...] = s0_ref[0]

    r = _iota((blk, blk), 0)
    c = _iota((blk, blk), 1)
    tril = jnp.logical_and(_div_pow2(r, CHUNK) == _div_pow2(c, CHUNK), r >= c)
    lane_lo = c < CHUNK
    row_lo = r < CHUNK
    dot = functools.partial(jnp.dot, preferred_element_type=F32)

    def block(ref, b0, cols, nchunk):
        return ref[b0:b0 + blk, cols] if nchunk == 2 else _pad_rows(ref[b0:b0 + CHUNK, cols], blk)

    for b0 in range(0, rows, blk):
        nchunk = min(blk, rows - b0) // CHUNK
        for hp in range(pairs):
            cols = slice(hp * LANES, (hp + 1) * LANES)
            qe = block(qe_ref, b0, cols, nchunk)
            keb = block(ke_ref, b0, cols, nchunk)
            kt_t = block(kt_ref, b0, cols, nchunk).T
            dec_t = block(dec_ref, b0, cols, nchunk).T
            dec_sw = pltpu.roll(dec_t, CHUNK, axis=1)
            decay = (jnp.where(lane_lo, dec_t, dec_sw), jnp.where(lane_lo, dec_sw, dec_t))
            vbs, qms, intra = [], [], []
            for hh in range(2):
                h = 2 * hp + hh
                vb = block(v_ref, b0, slice(h * LANES, (h + 1) * LANES), nchunk)
                qm = jnp.where(lane_lo if hh == 0 else jnp.logical_not(lane_lo), qe, jnp.zeros_like(qe))
                a = jnp.where(tril, _dot_nt(qm, keb), 0.0)
                intra.append(dot(a.astype(BF16), vb))
                vbs.append(vb)
                qms.append(qm)
            state = s_ref[hp]
            outs = [[], []]
            for ch in range(nchunk):
                sb = state.astype(BF16)
                rs = slice(ch * CHUNK, (ch + 1) * CHUNK)
                for hh in range(2):
                    outs[hh].append(dot(qms[hh][rs], sb) + intra[hh][rs])
                kt_c = jnp.where(lane_lo if ch == 0 else jnp.logical_not(lane_lo), kt_t, 0.0).astype(BF16)
                upd = jnp.where(row_lo, dot(kt_c, vbs[0]), dot(kt_c, vbs[1]))
                state = decay[ch] * state + upd
            s_ref[hp] = state
            for hh in range(2):
                h = 2 * hp + hh
                res = outs[hh][0] if nchunk == 1 else jnp.concatenate(outs[hh], axis=0)
                o_ref[b0:b0 + nchunk * CHUNK, h * LANES:(h + 1) * LANES] = res

    @pl.when(t == pl.num_programs(1) - 1)
    def _():
        sf_ref[0] = s_ref[...]


def _pad_rows(a, rows):
    return jnp.concatenate([a, jnp.zeros((rows - a.shape[0], a.shape[1]), a.dtype)], axis=0)


def _gla(qe, ke, kt, dec, v, s0, batch, seq):
    tg = _row_tile(seq, GLA_ROWS)
    nt = seq // tg
    pairs = GLA_HEADS // 2
    dk = GLA_HEADS * GLA_DK
    row = lambda width: pl.BlockSpec((tg, width), lambda b, t: (b * nt + t, 0))
    state = pl.BlockSpec((1, pairs, LANES, LANES), lambda b, t: (b, 0, 0, 0))
    return pl.pallas_call(
        _gla_kernel,
        out_shape=(jax.ShapeDtypeStruct((batch * seq, GLA_W), F32),
                   jax.ShapeDtypeStruct((batch, pairs, LANES, LANES), F32)),
        grid=(batch, nt),
        in_specs=[row(dk), row(dk), row(dk), row(dk), row(GLA_W), state],
        out_specs=(row(GLA_W), state),
        scratch_shapes=[pltpu.VMEM((pairs, LANES, LANES), F32)],
        compiler_params=_cparams(("arbitrary", "arbitrary")),
        name="gla_recurrence",
    )(qe, ke, kt, dec, v, s0)


def _merge_kernel(x_ref, of_ref, fg_ref, om_ref, mg_ref, og_ref, gg_ref, gn_ref, w_ref, lg_ref, lb_ref,
                  o_ref, y_ref, *, alpha):
    for h in range(FOX_HEADS):
        cols = slice(h * LANES, (h + 1) * LANES)
        y_ref[:, cols] = (of_ref[h].astype(F32) * _silu(fg_ref[:, cols].astype(F32))).astype(BF16)
    for h in range(MLA_HEADS):
        cols = slice(h * LANES, (h + 1) * LANES)
        y_ref[:, FOX_W + h * LANES:FOX_W + (h + 1) * LANES] = (
            om_ref[h].astype(F32) * _silu(mg_ref[:, cols].astype(F32))).astype(BF16)
    for h in range(GLA_HEADS):
        cols = slice(h * GLA_DV, (h + 1) * GLA_DV)
        og = og_ref[:, cols]
        og = og * lax.rsqrt(jnp.mean(og * og, axis=-1, keepdims=True) + RMS_EPS) * gn_ref[...]
        base = FOX_W + MLA_W + h * GLA_DV
        y_ref[:, base:base + GLA_DV] = (og * _silu(gg_ref[:, cols].astype(F32))).astype(BF16)
    z = alpha * x_ref[...] + jnp.dot(y_ref[...], w_ref[...], preferred_element_type=F32)
    mu = jnp.mean(z, axis=-1, keepdims=True)
    zc = z - mu
    var = jnp.mean(zc * zc, axis=-1, keepdims=True)
    o_ref[...] = zc * lax.rsqrt(var + LN_EPS) * lg_ref[...] + lb_ref[...]


def _merge(x2, o_fox, fg, o_mla, mg, o_gla, gg, gnorm, w_out, ln_g, ln_b, alpha, layer):
    rows, d = x2.shape
    tm = _row_tile(rows, MERGE_ROWS)
    row = lambda width: pl.BlockSpec((tm, width), lambda i: (i, 0))
    head = lambda n: pl.BlockSpec((n, tm, LANES), lambda i: (0, i, 0))
    const = lambda a: _layer_spec(a, layer)
    return pl.pallas_call(
        functools.partial(_merge_kernel, alpha=alpha),
        out_shape=jax.ShapeDtypeStruct((rows, d), F32),
        grid=(rows // tm,),
        in_specs=[row(d), head(FOX_HEADS), row(FOX_W), head(MLA_HEADS), row(MLA_W), row(GLA_W), row(GLA_W),
                  const(gnorm), const(w_out), const(ln_g), const(ln_b)],
        out_specs=row(d),
        scratch_shapes=[pltpu.VMEM((tm, d), BF16)],
        compiler_params=_cparams(("arbitrary",)),
        name="merge",
    )(x2, o_fox, fg, o_mla, mg, o_gla, gg, gnorm, w_out, ln_g, ln_b)


def _past_cumsum_kernel(lf_ref, o_ref):
    n, past = lf_ref.shape
    tl = min(past, 512)
    r = _iota((tl, tl), 0)
    c = _iota((tl, tl), 1)
    triu = jnp.where(r <= c, 1.0, 0.0).astype(BF16)
    d = functools.partial(jnp.dot, preferred_element_type=F32)
    carry = jnp.zeros((n, 1), F32)
    for j in range(0, past, tl):
        a1, a2, a3 = _split3(lf_ref[:, j:j + tl])
        cum = d(a1, triu) + d(a2, triu) + d(a3, triu) + carry
        o_ref[:, j:j + tl] = cum
        carry = cum[:, tl - 1:tl]
    o_ref[...] = (o_ref[...] - carry) * LOG2E


def _past_cumsum(lf_rows):
    n, past = lf_rows.shape
    tn = _row_tile(n, PROJ_ROWS)
    spec = pl.BlockSpec((tn, past), lambda i: (i, 0))
    return pl.pallas_call(
        _past_cumsum_kernel,
        out_shape=jax.ShapeDtypeStruct(lf_rows.shape, F32),
        grid=(n // tn,),
        in_specs=[spec],
        out_specs=spec,
        compiler_params=_cparams(("arbitrary",)),
        name="past_cumsum",
    )(lf_rows)


def _sample_softmax_step(s, v, m_ref, l_ref, acc_ref, h):
    m_prev = m_ref[h]
    m_next = jnp.maximum(m_prev, jnp.max(s, axis=1, keepdims=True))
    alpha = jnp.exp2(m_prev - m_next)
    p = jnp.exp2(s - m_next[:, :1])
    l_ref[h] = alpha * l_ref[h] + jnp.sum(p, axis=1, keepdims=True)
    acc_ref[h] = alpha * acc_ref[h] + jnp.dot(p.astype(BF16), v, preferred_element_type=F32)
    m_ref[h] = m_next


def _fox_sample_kernel(q_ref, ck_ref, cv_ref, cp_ref, kn_ref, vn_ref, cc_ref, cr_ref,
                       o_ref, m_ref, l_ref, acc_ref):
    j = pl.program_id(1)
    s_new = q_ref.shape[1]

    @pl.when(j == 0)
    def _():
        m_ref[...] = jnp.full_like(m_ref, -jnp.inf)
        l_ref[...] = jnp.zeros_like(l_ref)
        acc_ref[...] = jnp.zeros_like(acc_ref)

    for h in range(FOX_HEADS):
        s = _dot_nt(q_ref[h, :, :LANES], ck_ref[0, 0, h].astype(BF16))
        s = s + (cc_ref[:, h:h + 1] - cp_ref[0, h:h + 1, :])
        _sample_softmax_step(s, cv_ref[0, 0, h].astype(BF16), m_ref, l_ref, acc_ref, h)

    @pl.when(j == pl.num_programs(1) - 1)
    def _():
        keep = _iota((s_new, s_new), 1) <= _iota((s_new, s_new), 0)
        for h in range(FOX_HEADS):
            s = _dot_nt(q_ref[h, :, :LANES], kn_ref[h, :, :LANES])
            s = s + (cc_ref[:, h:h + 1] - cr_ref[0, h:h + 1, :])
            s = jnp.where(keep, s, -jnp.inf)
            _sample_softmax_step(s, vn_ref[h], m_ref, l_ref, acc_ref, h)
            o_ref[h] = (acc_ref[h] / l_ref[h]).astype(BF16)


def _fox_sample(q, cache_k, cache_v, c_past, k_new, v_new, ccol, crow, layer):
    _, batch, _, past, _ = cache_k.shape
    s_new = q.shape[1] // batch
    tk = _row_tile(past, SAMPLE_TK)
    new = lambda width: pl.BlockSpec((FOX_HEADS, s_new, width), lambda b, j: (0, b, 0))
    cache = pl.BlockSpec((1, 1, FOX_HEADS, tk, FOX_HD), lambda b, j: (layer, b, 0, j, 0))
    return pl.pallas_call(
        _fox_sample_kernel,
        out_shape=jax.ShapeDtypeStruct((FOX_HEADS, batch * s_new, FOX_HD), BF16),
        grid=(batch, past // tk),
        in_specs=[new(QK), cache, cache,
                  pl.BlockSpec((1, SUBLANES, tk), lambda b, j: (layer * batch + b, 0, j)),
                  new(QK), new(FOX_HD),
                  pl.BlockSpec((s_new, LANES), lambda b, j: (b, 0)),
                  pl.BlockSpec((1, SUBLANES, s_new), lambda b, j: (b, 0, 0))],
        out_specs=new(FOX_HD),
        scratch_shapes=[pltpu.VMEM((FOX_HEADS, s_new, LANES), F32),
                        pltpu.VMEM((FOX_HEADS, s_new, LANES), F32),
                        pltpu.VMEM((FOX_HEADS, s_new, FOX_HD), F32)],
        compiler_params=_cparams(("arbitrary", "arbitrary")),
        name="fox_sample_attn",
    )(q, cache_k, cache_v, c_past, k_new, v_new, ccol, crow)


def _mla_sample_kernel(q_ref, wuk_ref, wuv_ref, cc_ref, cr_ref, cn_ref, rn_ref,
                       o_ref, ql_ref, qr_ref, m_ref, l_ref, acc_ref):
    j = pl.program_id(1)
    s_new = q_ref.shape[1]
    dot = functools.partial(jnp.dot, preferred_element_type=F32)

    @pl.when(j == 0)
    def _():
        m_ref[...] = jnp.full_like(m_ref, -jnp.inf)
        l_ref[...] = jnp.zeros_like(l_ref)
        acc_ref[...] = jnp.zeros_like(acc_ref)
        for h in range(MLA_HEADS):
            rs = slice(h * s_new, (h + 1) * s_new)
            ql_ref[rs, :] = dot(q_ref[h, :, :LANES], wuk_ref[h]).astype(BF16)
            qr_ref[rs, :] = q_ref[h, :, LANES:]

    def update(c_keys, r_keys):
        s = _dot_nt(ql_ref[...], c_keys) + _dot_nt(qr_ref[...], r_keys)
        m_prev = m_ref[...]
        m_next = jnp.maximum(m_prev, jnp.max(s, axis=1, keepdims=True))
        alpha = jnp.exp2(m_prev - m_next)
        p = jnp.exp2(s - m_next[:, :1])
        l_ref[...] = alpha * l_ref[...] + jnp.sum(p, axis=1, keepdims=True)
        acc_ref[...] = alpha[:, :1] * acc_ref[...] + dot(p.astype(BF16), c_keys)
        m_ref[...] = m_next

    update(cc_ref[0, 0].astype(BF16), cr_ref[0, 0].astype(BF16))

    @pl.when(j == pl.num_programs(1) - 1)
    def _():
        rn = jnp.where(_iota((s_new, LANES), 1) < MLA_ROPE, rn_ref[...], 0.0)
        update(cn_ref[...].astype(BF16), rn.astype(BF16))
        for h in range(MLA_HEADS):
            rs = slice(h * s_new, (h + 1) * s_new)
            o_lat = (acc_ref[rs, :] / l_ref[rs, :1]).astype(BF16)
            o_ref[h] = dot(o_lat, wuv_ref[h]).astype(BF16)


def _mla_sample(q, wuk, wuv, cache_c, cache_r, c_new, r_new, layer):
    _, batch, past, _ = cache_c.shape
    s_new = q.shape[1] // batch
    tk = _row_tile(past, SAMPLE_TK)
    rows = MLA_HEADS * s_new
    new = lambda width: pl.BlockSpec((s_new, width), lambda b, j: (b, 0))
    head = lambda width: pl.BlockSpec((MLA_HEADS, s_new, width), lambda b, j: (0, b, 0))
    const = lambda a: _layer_spec(a, layer)
    return pl.pallas_call(
        _mla_sample_kernel,
        out_shape=jax.ShapeDtypeStruct((MLA_HEADS, batch * s_new, MLA_V), BF16),
        grid=(batch, past // tk),
        in_specs=[head(QK), const(wuk), const(wuv),
                  pl.BlockSpec((1, 1, tk, MLA_KV_RANK), lambda b, j: (layer, b, j, 0)),
                  pl.BlockSpec((1, 1, tk, LANES), lambda b, j: (layer, b, j, 0)),
                  pl.BlockSpec((None, s_new, MLA_KV_RANK), lambda b, j: (layer, b, 0)), new(LANES)],
        out_specs=head(MLA_V),
        scratch_shapes=[pltpu.VMEM((rows, MLA_KV_RANK), BF16),
                        pltpu.VMEM((rows, LANES), BF16),
                        pltpu.VMEM((rows, LANES), F32),
                        pltpu.VMEM((rows, LANES), F32),
                        pltpu.VMEM((rows, MLA_KV_RANK), F32)],
        compiler_params=_cparams(("arbitrary", "arbitrary")),
        name="mla_sample_attn",
    )(q, wuk, wuv, cache_c, cache_r, c_new, r_new)


def _pad_cols(a, width):
    return jnp.pad(a, ((0, 0), (0, width - a.shape[1])))


W_FOX_COLS = 4 * FOX_W + LANES
W_MLA_COLS = MLA_HEADS * QK + MLA_KV_RANK + LANES + MLA_W
W_GLA_COLS = 2 * GLA_HEADS * GLA_DK + GLA_W + LANES + GLA_W
W_IN_BLOCK_COLS = -(-IN_OFFS[-1] // LANES) * LANES
PACK_ROWS = 256


def _pack_kernel(w_ref, fox_ref, mla_ref, gla_ref):
    rows = w_ref.shape[1]
    lane = _iota((rows, LANES), 1)

    def take(src, width):
        start = (src // LANES) * LANES
        shift = src - start
        span = -(-(shift + width) // LANES) * LANES
        win = w_ref[0, :, start:start + span]
        if shift:
            win = pltpu.roll(win, span - shift, axis=1)
        return win[:, :width]

    def narrow(src, width):
        return jnp.where(lane < width, take(src, LANES), 0.0)

    def rope_pair(src):
        p = take(src, LANES)
        quarter = MLA_ROPE // 2
        return jnp.where(lane < MLA_ROPE, p,
                         jnp.where(lane < MLA_ROPE + quarter, pltpu.roll(p, quarter, axis=1),
                                   pltpu.roll(p, LANES - quarter, axis=1)))

    def put(ref, col, val):
        ref[0, :, col:col + val.shape[1]] = val.astype(BF16)

    fq, fk, fv, ff, fg, mq, mc, mr, mg, gq, gk, gv, ggk, gg = IN_OFFS[:-1]
    put(fox_ref, 0, take(fq, FOX_W))
    put(fox_ref, FOX_W, take(fk, FOX_W))
    put(fox_ref, 2 * FOX_W, take(fv, FOX_W))
    put(fox_ref, 3 * FOX_W, take(fg, FOX_W))
    put(fox_ref, 4 * FOX_W, narrow(ff, FOX_HEADS))
    for h in range(MLA_HEADS):
        src = mq + h * (MLA_NOPE + MLA_ROPE)
        put(mla_ref, h * QK, take(src, MLA_NOPE))
        put(mla_ref, h * QK + LANES, rope_pair(src + MLA_NOPE))
    col = MLA_HEADS * QK
    put(mla_ref, col, take(mc, MLA_KV_RANK))
    put(mla_ref, col + MLA_KV_RANK, rope_pair(mr))
    put(mla_ref, col + MLA_KV_RANK + LANES, take(mg, MLA_W))
    dk = GLA_HEADS * GLA_DK
    put(gla_ref, 0, take(gq, dk))
    put(gla_ref, dk, take(gk, dk))
    put(gla_ref, 2 * dk, take(gv, GLA_W))
    put(gla_ref, 2 * dk + GLA_W, narrow(ggk, GLA_GATE_RANK))
    put(gla_ref, 2 * dk + GLA_W + LANES, take(gg, GLA_W))


def _pack_w_in(w_in):
    depth, d, _ = w_in.shape
    tr = _row_tile(d, PACK_ROWS)
    out = lambda cols: pl.BlockSpec((1, tr, cols), lambda l, i: (l, i, 0))
    return pl.pallas_call(
        _pack_kernel,
        out_shape=(jax.ShapeDtypeStruct((depth, d, W_FOX_COLS), BF16),
                   jax.ShapeDtypeStruct((depth, d, W_MLA_COLS), BF16),
                   jax.ShapeDtypeStruct((depth, d, W_GLA_COLS), BF16)),
        grid=(depth, d // tr),
        in_specs=[pl.BlockSpec((1, tr, W_IN_BLOCK_COLS), lambda l, i: (l, i, 0))],
        out_specs=(out(W_FOX_COLS), out(W_MLA_COLS), out(W_GLA_COLS)),
        compiler_params=_cparams(("arbitrary", "arbitrary")),
        name="pack_w_in",
    )(w_in)


def _stacked_params(w_in, b_f, kv_norm, w_up, w_gk_up, b_gk, gla_norm, w_out, ln_g, ln_b):
    depth = w_in.shape[0]
    w_fox, w_mla, w_gla = _pack_w_in(w_in)
    w_up_k = w_up[:, :, :, :MLA_NOPE]
    w_up_v = w_up[:, :, :, MLA_NOPE:]
    return dict(
        w_fox=w_fox,
        b_f=jnp.pad(b_f, ((0, 0), (0, LANES - FOX_HEADS)))[:, None, :],
        w_mla=w_mla,
        kv_norm=kv_norm[:, None, :],
        w_up=jnp.concatenate([w_up_k.reshape(depth, MLA_KV_RANK, MLA_W),
                              w_up_v.reshape(depth, MLA_KV_RANK, MLA_W)], axis=2).astype(BF16),
        w_uk=jnp.transpose(w_up_k, (0, 2, 3, 1)).astype(BF16),
        w_uv=jnp.transpose(w_up_v, (0, 2, 1, 3)).astype(BF16),
        w_gla=w_gla,
        w_gk=jnp.pad(w_gk_up, ((0, 0), (0, LANES - GLA_GATE_RANK), (0, 0))).astype(BF16),
        b_gk=b_gk[:, None, :],
        gla_norm=gla_norm[:, None, :],
        w_out=w_out.astype(BF16),
        ln_g=ln_g[:, None, :],
        ln_b=ln_b[:, None, :],
    )


def _rope_table(pos, rows):
    half = MLA_ROPE // 2
    inv = ROPE_BASE ** (-jnp.arange(half, dtype=F32) / half)
    ang = pos.astype(F32)[:, None] * inv[None, :]
    cos, sin = jnp.cos(ang), jnp.sin(ang)
    tab = jnp.concatenate([cos, cos, -sin, sin], axis=1)
    reps = max(rows // tab.shape[0], 1)
    return jnp.tile(tab, (reps, 1))


def _branches(x2, p, sel, tab, batch, seq, with_kv, layer, prev):
    fox = _fox_proj(x2, p["w_fox"], p["b_f"], sel, batch, seq, layer, prev and prev[:3])
    mla = _mla_proj(x2, p["w_mla"], tab, p["kv_norm"], p["w_up"], with_kv, layer, prev and prev[3:])
    gla = _gla_proj(x2, p["w_gla"], p["w_gk"], p["b_gk"], layer)
    return fox, mla, gla


def kernel(x_prompt, x_sample, cache_fox_k, cache_fox_v, cache_fox_logf, cache_mla_ckv, cache_mla_krope,
           state_gla, w_in, b_fox_f, mla_kv_norm, w_mla_kv_up, w_gla_gk_up, b_gla_gk, gla_norm, w_out,
           ln_g, ln_b):
    depth = w_in.shape[0]
    bp, tp, d = x_prompt.shape
    bs, ts, _ = x_sample.shape
    past = cache_fox_k.shape[2]
    alpha = (2 * depth) ** 0.25
    pairs = GLA_HEADS // 2

    tab_p = _rope_table(jnp.arange(tp), _row_tile(bp * tp, PROJ_ROWS))
    tab_s = _rope_table(past + jnp.arange(ts), _row_tile(bs * ts, PROJ_ROWS))
    sel = _fox_bias_selectors()

    cache_k = jnp.transpose(cache_fox_k, (0, 1, 3, 2, 4))
    cache_v = jnp.transpose(cache_fox_v, (0, 1, 3, 2, 4))
    lf_rows = jnp.pad(jnp.swapaxes(cache_fox_logf.astype(F32), 2, 3),
                      ((0, 0), (0, 0), (0, SUBLANES - FOX_HEADS), (0, 0)))
    c_past = _past_cumsum(lf_rows.reshape(depth * bs * SUBLANES, past)).reshape(depth * bs, SUBLANES, past)
    cache_r = jnp.pad(cache_mla_krope, ((0, 0), (0, 0), (0, 0), (0, LANES - MLA_ROPE)))

    xp = x_prompt.reshape(bp * tp, d)
    xs = x_sample.reshape(bs * ts, d)
    zero_state = jnp.zeros((bp, pairs, LANES, LANES), F32)
    state_s = state_gla.reshape(depth, bs, pairs, LANES, LANES)
    p = _stacked_params(w_in, b_fox_f, mla_kv_norm, w_mla_kv_up, w_gla_gk_up, b_gla_gk, gla_norm, w_out,
                        ln_g, ln_b)
    merge_args = (p["gla_norm"], p["w_out"], p["ln_g"], p["ln_b"], alpha)
    kept_p = kept_s = None
    gla_p, gla_s = [], []

    for l in range(depth):
        (fk, fv, flf, fq, fkc, fvb, fg, _, _), (mc, mkr, mq, _, mg, mkc, mv), (*gla_in, gg) = \
            _branches(xp, p, sel, tab_p, bp, tp, True, l, kept_p)
        kept_p = (fk, fv, flf, mc, mkr)
        o_f = _prompt_attn(fq, fkc, fvb, bp, tp, chunked=False)
        o_m = _prompt_attn(mq, mkc, mv, bp, tp, chunked=True)
        o_g, s_p = _gla(*gla_in, zero_state, bp, tp)
        gla_p.append(s_p.reshape(bp, GLA_HEADS, GLA_DK, GLA_DV))
        xp = _merge(xp, o_f, fg, o_m, mg, o_g, gg, *merge_args, l)

        (fk, fv, flf, fq, fkc, fvb, fg, fcc, fcr), (mc, mkr, mq, mkr2, mg), (*gla_in, gg) = \
            _branches(xs, p, sel, tab_s, bs, ts, False, l, kept_s)
        kept_s = (fk, fv, flf, mc, mkr)
        crow_s = jnp.swapaxes(fcr.reshape(SUBLANES, bs, ts), 0, 1)
        o_f = _fox_sample(fq, cache_k, cache_v, c_past, fkc, fvb, fcc, crow_s, l)
        o_m = _mla_sample(mq, p["w_uk"], p["w_uv"], cache_mla_ckv, cache_r, mc, mkr2, l)
        o_g, s_s = _gla(*gla_in, state_s[l], bs, ts)
        gla_s.append(s_s.reshape(bs, GLA_HEADS, GLA_DK, GLA_DV))
        xs = _merge(xs, o_f, fg, o_m, mg, o_g, gg, *merge_args, l)

    def returned(kept, states, batch, seq):
        fk, fv, flf, mc, mkr = kept
        return (jnp.transpose(fk, (0, 1, 3, 2, 4)), jnp.transpose(fv, (0, 1, 3, 2, 4)),
                flf[:, :, :FOX_HEADS].reshape(depth, batch, seq, FOX_HEADS),
                mc.reshape(depth, batch, seq, MLA_KV_RANK), mkr.reshape(depth, batch, seq, MLA_ROPE),
                jnp.stack(states))

    return ((xp.reshape(bp, tp, d), xs.reshape(bs, ts, d))
            + returned(kept_p, gla_p, bp, tp) + returned(kept_s, gla_s, bs, ts))
```

```python
import functools
import math

import numpy as np
import jax
import jax.numpy as jnp
from jax import lax
from jax.experimental import pallas as pl
from jax.experimental.pallas import tpu as pltpu

F32 = jnp.float32
BF16 = jnp.bfloat16

LANES = 128
SUBLANES = 8
VMEM_LIMIT_BYTES = 60 * 1024 * 1024

CHUNK = 64
FOX_HEADS, FOX_HD = 6, 128
MLA_HEADS, MLA_NOPE, MLA_ROPE, MLA_V, MLA_KV_RANK = 6, 128, 64, 128, 512
GLA_HEADS, GLA_DK, GLA_DV, GLA_GATE_RANK = 4, 64, 128, 16
GLA_GATE_NORM = 16.0
FOX_W = FOX_HEADS * FOX_HD
MLA_W = MLA_HEADS * MLA_V
GLA_W = GLA_HEADS * GLA_DV
QK = 2 * LANES
ROPE_BASE = 10000.0
LN_EPS = 1e-5
RMS_EPS = 1e-6
LOG2E = math.log2(math.e)
FOX_SCALE = FOX_HD ** -0.5
MLA_SCALE = (MLA_NOPE + MLA_ROPE) ** -0.5
GLA_SCALE = GLA_DK ** -0.5

IN_SIZES = (FOX_W, FOX_W, FOX_W, FOX_HEADS, FOX_W,
            MLA_HEADS * (MLA_NOPE + MLA_ROPE), MLA_KV_RANK, MLA_ROPE, MLA_W,
            GLA_HEADS * GLA_DK, GLA_HEADS * GLA_DK, GLA_W, GLA_GATE_RANK, GLA_W)
IN_OFFS = tuple(int(v) for v in np.concatenate([[0], np.cumsum(IN_SIZES)]))

PROJ_ROWS = 512
MERGE_ROWS = 512
ATTN_TQ = 1024
ATTN_TK = 1024
GLA_ROWS = 256
SAMPLE_TK = 2048


def _cparams(sem):
    return pltpu.CompilerParams(dimension_semantics=sem, vmem_limit_bytes=VMEM_LIMIT_BYTES)


def _row_tile(rows, want):
    t = min(rows, want)
    assert rows % t == 0, (rows, t)
    return t


def _split3(a):
    a1 = a.astype(BF16)
    r1 = a - a1.astype(F32)
    a2 = r1.astype(BF16)
    a3 = (r1 - a2.astype(F32)).astype(BF16)
    return a1, a2, a3


def _mask_dot(mask_bf16, a):
    a1, a2, a3 = _split3(a)
    d = functools.partial(jnp.dot, preferred_element_type=F32)
    return d(mask_bf16, a1) + d(mask_bf16, a2) + d(mask_bf16, a3)


def _dot_nt(a, b):
    return lax.dot_general(a, b, (((1,), (1,)), ((), ())), preferred_element_type=F32)


def _log_sigmoid(z):
    return jnp.minimum(z, 0.0) - jnp.log1p(jnp.exp(-jnp.abs(z)))


def _silu(z):
    return z / (1.0 + jnp.exp(-z))


def _iota(shape, dim):
    return lax.broadcasted_iota(jnp.int32, shape, dim)


def _div_pow2(x, n):
    assert n > 0 and n & (n - 1) == 0, n
    return lax.shift_right_logical(x, jnp.int32(n.bit_length() - 1))


def _layer_spec(a, layer):
    zeros = (0,) * (a.ndim - 1)
    return pl.BlockSpec((None,) + a.shape[1:], lambda *_: (layer,) + zeros)


def _stacked_outputs(prev, n_inputs, first_out):
    if prev is None:
        return [], [], {}
    specs = [pl.BlockSpec(memory_space=pl.ANY)] * len(prev)
    return list(prev), specs, {n_inputs + j: first_out + j for j in range(len(prev))}


def _lane_tile(a, width):
    return jnp.concatenate([a] * (width // LANES), axis=1)


def _fox_bias_selectors():
    sq = np.zeros((3 * LANES, FOX_W), np.float32)
    sk = np.zeros((3 * LANES, FOX_W), np.float32)
    oq = np.zeros((1, FOX_W), np.float32)
    ok = np.zeros((1, FOX_W), np.float32)
    for h in range(FOX_HEADS):
        for piece in range(3):
            sq[piece * LANES + h, h * LANES + piece] = 1.0
            sk[piece * LANES + h, h * LANES + 3 + piece] = -1.0
            oq[0, h * LANES + 3 + piece] = 1.0
            ok[0, h * LANES + piece] = 1.0
    return jnp.asarray(sq, BF16), jnp.asarray(sk, BF16), jnp.asarray(oq), jnp.asarray(ok)


def _fox_proj_kernel(x_ref, w_ref, bf_ref, sq_ref, sk_ref, oq_ref, ok_ref, *refs, seq, tiles_per_seq):
    kf_ref, vf_ref, lf_ref, q_ref, k_ref, v_ref, g_ref, cc_ref, cr_ref, carry_ref = refs[-10:]
    tm = x_ref.shape[0]
    nb, _, tt, _ = kf_ref.shape
    xb = x_ref[...].astype(BF16)
    dot = functools.partial(jnp.dot, preferred_element_type=F32)

    def proj(a, b):
        return dot(xb, w_ref[:, a:b])

    q = proj(0, FOX_W) * (FOX_SCALE * LOG2E)
    k = proj(FOX_W, 2 * FOX_W)
    v = proj(2 * FOX_W, 3 * FOX_W)
    g_ref[...] = proj(3 * FOX_W, 4 * FOX_W).astype(BF16)
    lf = _log_sigmoid(proj(4 * FOX_W, 4 * FOX_W + LANES) + bf_ref[...])
    lf_ref[...] = lf

    r = _iota((tm, tm), 0)
    c = _iota((tm, tm), 1)
    tri = r >= c
    if seq < tm:
        tri = jnp.logical_and(tri, _div_pow2(r, seq) == _div_pow2(c, seq))
    c_tile = _mask_dot(jnp.where(tri, 1.0, 0.0).astype(BF16), lf)
    if seq > tm:
        @pl.when(pl.program_id(0) % tiles_per_seq == 0)
        def _():
            carry_ref[...] = jnp.zeros_like(carry_ref)
        c_tile = c_tile + carry_ref[...]
        carry_ref[...] = c_tile[tm - 1:tm, :]
    c2 = c_tile * LOG2E
    cc_ref[...] = c2
    cr_ref[...] = c2.T[:SUBLANES, :]

    pieces = jnp.concatenate(_split3(c2), axis=1)
    qx = dot(pieces, sq_ref[...]) + oq_ref[...]
    kx = dot(pieces, sk_ref[...]) + ok_ref[...]
    for h in range(FOX_HEADS):
        cols = slice(h * LANES, (h + 1) * LANES)
        q_ref[h, :, :LANES] = q[:, cols].astype(BF16)
        q_ref[h, :, LANES:] = qx[:, cols].astype(BF16)
        k_ref[h, :, :LANES] = k[:, cols].astype(BF16)
        k_ref[h, :, LANES:] = kx[:, cols].astype(BF16)
        v_ref[h] = v[:, cols].astype(BF16)
        for b in range(nb):
            kf_ref[b, h] = k[b * tt:(b + 1) * tt, cols]
            vf_ref[b, h] = v[b * tt:(b + 1) * tt, cols]


def _fox_proj(x2, w, bf, sel, batch, seq, layer, prev):
    depth = w.shape[0]
    rows, d = x2.shape
    tm = _row_tile(rows, PROJ_ROWS)
    assert seq % tm == 0 or tm % seq == 0
    tiles_per_seq = max(seq // tm, 1)
    nb, tt = max(tm // seq, 1), min(tm, seq)
    row = lambda width: pl.BlockSpec((tm, width), lambda i: (i, 0))
    head = lambda width: pl.BlockSpec((FOX_HEADS, tm, width), lambda i: (0, i, 0))
    const = lambda a: pl.BlockSpec(a.shape, lambda i: (0, 0))
    cache = pl.BlockSpec((None, nb, FOX_HEADS, tt, FOX_HD),
                         lambda i: (layer, i // tiles_per_seq, 0, i % tiles_per_seq, 0))
    out_shape = (
        jax.ShapeDtypeStruct((depth, batch, FOX_HEADS, seq, FOX_HD), F32),
        jax.ShapeDtypeStruct((depth, batch, FOX_HEADS, seq, FOX_HD), F32),
        jax.ShapeDtypeStruct((depth, rows, LANES), F32),
        jax.ShapeDtypeStruct((FOX_HEADS, rows, QK), BF16),
        jax.ShapeDtypeStruct((FOX_HEADS, rows, QK), BF16),
        jax.ShapeDtypeStruct((FOX_HEADS, rows, FOX_HD), BF16),
        jax.ShapeDtypeStruct((rows, FOX_W), BF16),
        jax.ShapeDtypeStruct((rows, LANES), F32),
        jax.ShapeDtypeStruct((SUBLANES, rows), F32),
    )
    out_specs = (cache, cache, pl.BlockSpec((None, tm, LANES), lambda i: (layer, i, 0)),
                 head(QK), head(QK), head(FOX_HD), row(FOX_W), row(LANES),
                 pl.BlockSpec((SUBLANES, tm), lambda i: (0, i)))
    in_specs = [row(d), _layer_spec(w, layer), _layer_spec(bf, layer)] + [const(a) for a in sel]
    prev_args, prev_specs, aliases = _stacked_outputs(prev, len(in_specs), 0)
    return pl.pallas_call(
        functools.partial(_fox_proj_kernel, seq=seq, tiles_per_seq=tiles_per_seq),
        out_shape=out_shape,
        grid=(rows // tm,),
        in_specs=in_specs + prev_specs,
        out_specs=out_specs,
        scratch_shapes=[pltpu.VMEM((1, LANES), F32)],
        input_output_aliases=aliases,
        compiler_params=_cparams(("arbitrary",)),
        name="fox_proj",
    )(x2, w, bf, *sel, *prev_args)


def _rope_block(blk, tab):
    t = blk * tab
    return t + pltpu.roll(t, LANES // 2, axis=1)


def _mla_proj_kernel(*refs, with_kv, n_prev):
    if with_kv:
        x_ref, w_ref, tab_ref, kvn_ref, wup_ref = refs[:5]
        c_ref, kr_ref, q_ref, kr2_ref, g_ref, kc_ref, v_ref = refs[5 + n_prev:]
    else:
        x_ref, w_ref, tab_ref, kvn_ref = refs[:4]
        c_ref, kr_ref, q_ref, kr2_ref, g_ref = refs[4 + n_prev:]
    tm = x_ref.shape[0]
    xb = x_ref[...].astype(BF16)
    tab = tab_ref[...]
    dot = functools.partial(jnp.dot, preferred_element_type=F32)

    def proj(a, b):
        return dot(xb, w_ref[:, a:b])

    for h in range(MLA_HEADS):
        hq = proj(h * QK, (h + 1) * QK) * (MLA_SCALE * LOG2E)
        q_ref[h, :, :LANES] = hq[:, :LANES].astype(BF16)
        q_ref[h, :, LANES:] = _rope_block(hq[:, LANES:], tab).astype(BF16)
    off = MLA_HEADS * QK
    mc = proj(off, off + MLA_KV_RANK)
    c = mc * lax.rsqrt(jnp.mean(mc * mc, axis=-1, keepdims=True) + RMS_EPS) * kvn_ref[...]
    c_ref[...] = c
    off += MLA_KV_RANK
    kr = _rope_block(proj(off, off + LANES), tab)
    kr_ref[...] = kr[:, :MLA_ROPE]
    kr2_ref[...] = kr
    off += LANES
    g_ref[...] = proj(off, off + MLA_W).astype(BF16)
    if with_kv:
        cb = c.astype(BF16)
        krz = jnp.where(_iota((tm, LANES), 1) < MLA_ROPE, kr, 0.0).astype(BF16)
        for h in range(MLA_HEADS):
            kc_ref[h, :, :LANES] = dot(cb, wup_ref[:, h * LANES:(h + 1) * LANES]).astype(BF16)
            kc_ref[h, :, LANES:] = krz
            v_ref[h] = dot(cb, wup_ref[:, MLA_W + h * LANES:MLA_W + (h + 1) * LANES]).astype(BF16)


def _mla_proj(x2, w, tab, kvn, wup, with_kv, layer, prev):
    depth = w.shape[0]
    rows, d = x2.shape
    tm = _row_tile(rows, PROJ_ROWS)
    tab_tiles = tab.shape[0] // tm
    row = lambda width: pl.BlockSpec((tm, width), lambda i: (i, 0))
    head = lambda width: pl.BlockSpec((MLA_HEADS, tm, width), lambda i: (0, i, 0))
    stacked = lambda width: pl.BlockSpec((None, tm, width), lambda i: (layer, i, 0))
    in_specs = [row(d), _layer_spec(w, layer),
                pl.BlockSpec((tm, LANES), lambda i: (i % tab_tiles, 0)),
                _layer_spec(kvn, layer)]
    args = [x2, w, tab, kvn]
    out_shape = [jax.ShapeDtypeStruct((depth, rows, MLA_KV_RANK), F32),
                 jax.ShapeDtypeStruct((depth, rows, MLA_ROPE), F32),
                 jax.ShapeDtypeStruct((MLA_HEADS, rows, QK), BF16),
                 jax.ShapeDtypeStruct((rows, LANES), F32),
                 jax.ShapeDtypeStruct((rows, MLA_W), BF16)]
    out_specs = [stacked(MLA_KV_RANK), stacked(MLA_ROPE), head(QK), row(LANES), row(MLA_W)]
    if with_kv:
        in_specs.append(_layer_spec(wup, layer))
        args.append(wup)
        out_shape += [jax.ShapeDtypeStruct((MLA_HEADS, rows, QK), BF16),
                      jax.ShapeDtypeStruct((MLA_HEADS, rows, MLA_V), BF16)]
        out_specs += [head(QK), head(MLA_V)]
    prev_args, prev_specs, aliases = _stacked_outputs(prev, len(in_specs), 0)
    return pl.pallas_call(
        functools.partial(_mla_proj_kernel, with_kv=with_kv, n_prev=len(prev_args)),
        out_shape=tuple(out_shape),
        grid=(rows // tm,),
        in_specs=in_specs + prev_specs,
        out_specs=tuple(out_specs),
        input_output_aliases=aliases,
        compiler_params=_cparams(("arbitrary",)),
        name="mla_proj_kv" if with_kv else "mla_proj",
    )(*args, *prev_args)


def _gla_proj_kernel(x_ref, w_ref, wgk_ref, bgk_ref, qe_ref, ke_ref, kt_ref, dec_ref, v_ref, g_ref):
    tm = x_ref.shape[0]
    xb = x_ref[...].astype(BF16)

    def proj(a, b):
        return jnp.dot(xb, w_ref[:, a:b], preferred_element_type=F32)

    dk = GLA_HEADS * GLA_DK
    q = proj(0, dk)
    k = proj(dk, 2 * dk)
    v_ref[...] = proj(2 * dk, 2 * dk + GLA_W).astype(BF16)
    off = 2 * dk + GLA_W
    low = proj(off, off + LANES).astype(BF16)
    z = jnp.dot(low, wgk_ref[...], preferred_element_type=F32) + bgk_ref[...]
    gate = _log_sigmoid(z) * (1.0 / GLA_GATE_NORM)
    off += LANES
    g_ref[...] = proj(off, off + GLA_W).astype(BF16)

    r = _iota((tm, tm), 0)
    c = _iota((tm, tm), 1)
    tril = jnp.logical_and(_div_pow2(r, CHUNK) == _div_pow2(c, CHUNK), r >= c)
    bcum = _mask_dot(jnp.where(tril, 1.0, 0.0).astype(BF16), gate)
    btot = jnp.concatenate(
        [jnp.broadcast_to(bcum[e - 1:e, :], (CHUNK, dk)) for e in range(CHUNK, tm + 1, CHUNK)], axis=0)
    qe_ref[...] = (q * jnp.exp(bcum) * GLA_SCALE).astype(BF16)
    ke_ref[...] = (k * jnp.exp(-bcum)).astype(BF16)
    kt_ref[...] = k * jnp.exp(btot - bcum)
    dec_ref[...] = jnp.exp(btot)


def _gla_proj(x2, w, wgk, bgk, layer):
    rows, d = x2.shape
    tm = _row_tile(rows, PROJ_ROWS)
    dk = GLA_HEADS * GLA_DK
    row = lambda width: pl.BlockSpec((tm, width), lambda i: (i, 0))
    const = lambda a: _layer_spec(a, layer)
    return pl.pallas_call(
        _gla_proj_kernel,
        out_shape=(jax.ShapeDtypeStruct((rows, dk), BF16),
                   jax.ShapeDtypeStruct((rows, dk), BF16),
                   jax.ShapeDtypeStruct((rows, dk), F32),
                   jax.ShapeDtypeStruct((rows, dk), F32),
                   jax.ShapeDtypeStruct((rows, GLA_W), BF16),
                   jax.ShapeDtypeStruct((rows, GLA_W), BF16)),
        grid=(rows // tm,),
        in_specs=[row(d), const(w), const(wgk), const(bgk)],
        out_specs=(row(dk), row(dk), row(dk), row(dk), row(GLA_W), row(GLA_W)),
        compiler_params=_cparams(("arbitrary",)),
        name="gla_proj",
    )(x2, w, wgk, bgk)


def _online_softmax_update(s, v, m_prev, acc_prev):
    tk = s.shape[1]
    m_next = jnp.maximum(m_prev, jnp.max(s, axis=1, keepdims=True))
    alpha = jnp.exp2(m_prev - m_next)
    p = jnp.exp2(s - _lane_tile(m_next, tk)).astype(BF16)
    v_ones = jnp.concatenate([v, jnp.ones((tk, LANES), BF16)], axis=1)
    acc = _lane_tile(alpha, QK) * acc_prev + jnp.dot(p, v_ones, preferred_element_type=F32)
    return m_next, acc


def _prompt_attn_kernel(qi_ref, ki_ref, first_ref, last_ref, mask_ref,
                        q_ref, k_ref, v_ref, o_ref, m_ref, acc_ref, bias_ref, s_ref, *, chunked):
    s_idx = pl.program_id(1)
    heads, tq, _ = q_ref.shape
    tk = k_ref.shape[1]
    half = tq // 2
    lo, hi = slice(0, half), slice(half, tq)

    @pl.when(first_ref[s_idx] == 1)
    def _():
        m_ref[...] = jnp.full_like(m_ref, -jnp.inf)
        acc_ref[...] = jnp.zeros_like(acc_ref)

    def sweep(masked):
        keys_lo = half if (masked and tq == tk) else tk

        def scores(h, rows, slot, nkeys):
            s = _dot_nt(q_ref[h, rows, :], k_ref[h, :nkeys, :])
            if masked:
                s = s + bias_ref[rows, :nkeys]
            s_ref[slot, :, :nkeys] = s

        def softmax_pv(h, rows, slot, nkeys):
            m_next, acc = _online_softmax_update(s_ref[slot, :, :nkeys], v_ref[h, :nkeys, :],
                                                 m_ref[h, rows, :], acc_ref[h, rows, :])
            m_ref[h, rows, :] = m_next
            acc_ref[h, rows, :] = acc

        scores(0, lo, 0, keys_lo)
        for h in range(heads):
            scores(h, hi, 1, tk)
            softmax_pv(h, lo, 0, keys_lo)
            if h + 1 < heads:
                scores(h + 1, lo, 0, keys_lo)
            softmax_pv(h, hi, 1, tk)

    @pl.when(mask_ref[s_idx] == 1)
    def _():
        qpos = qi_ref[s_idx] * tq + _iota((tq, tk), 0)
        kpos = ki_ref[s_idx] * tk + _iota((tq, tk), 1)
        if chunked:
            keep = _div_pow2(kpos, CHUNK) <= _div_pow2(qpos, CHUNK)
        else:
            keep = kpos <= qpos
        bias_ref[...] = jnp.where(keep, 0.0, -jnp.inf)
        sweep(True)

    @pl.when(mask_ref[s_idx] == 0)
    def _():
        sweep(False)

    @pl.when(last_ref[s_idx] == 1)
    def _():
        for h in range(heads):
            o_ref[h] = (acc_ref[h, :, :LANES] / acc_ref[h, :, LANES:]).astype(BF16)


def _attn_schedule(seq, tq, tk, chunked):
    qi, ki, first, last, mask = [], [], [], [], []
    for a in range(seq // tq):
        q_lo, q_hi = a * tq, (a + 1) * tq - 1
        if chunked:
            q_lo, q_hi = (q_lo // CHUNK) * CHUNK, (q_hi // CHUNK) * CHUNK + CHUNK - 1
        tiles = [b for b in range(seq // tk) if b * tk <= q_hi]
        for b in tiles:
            qi.append(a)
            ki.append(b)
            first.append(int(b == tiles[0]))
            last.append(int(b == tiles[-1]))
            mask.append(int((b + 1) * tk - 1 > q_lo))
    return [jnp.asarray(np.asarray(t, np.int32)) for t in (qi, ki, first, last, mask)]


def _prompt_attn(q, k, v, batch, seq, *, chunked):
    heads = q.shape[0]
    tq = _row_tile(seq, ATTN_TQ)
    tk = _row_tile(seq, ATTN_TK)
    nq, nk = seq // tq, seq // tk
    tabs = _attn_schedule(seq, tq, tk, chunked)
    steps = int(tabs[0].shape[0])
    qmap = lambda b, s, qi, ki, *_: (0, b * nq + qi[s], 0)
    kmap = lambda b, s, qi, ki, *_: (0, b * nk + ki[s], 0)
    return pl.pallas_call(
        functools.partial(_prompt_attn_kernel, chunked=chunked),
        out_shape=jax.ShapeDtypeStruct((heads, batch * seq, LANES), BF16),
        grid_spec=pltpu.PrefetchScalarGridSpec(
            num_scalar_prefetch=5,
            grid=(batch, steps),
            in_specs=[pl.BlockSpec((heads, tq, QK), qmap),
                      pl.BlockSpec((heads, tk, QK), kmap),
                      pl.BlockSpec((heads, tk, LANES), kmap)],
            out_specs=pl.BlockSpec((heads, tq, LANES), qmap),
            scratch_shapes=[pltpu.VMEM((heads, tq, LANES), F32),
                            pltpu.VMEM((heads, tq, QK), F32),
                            pltpu.VMEM((tq, tk), F32),
                            pltpu.VMEM((2, tq // 2, tk), F32)]),
        compiler_params=_cparams(("arbitrary", "arbitrary")),
        name="mla_attn" if chunked else "fox_attn",
    )(*tabs, q, k, v)


def _gla_kernel(qe_ref, ke_ref, kt_ref, dec_ref, v_ref, s0_ref, o_ref, sf_ref, s_ref):
    t = pl.program_id(1)
    rows = qe_ref.shape[0]
    blk = 2 * CHUNK
    pairs = GLA_HEADS // 2

    @pl.when(t == 0)
    def _():
        s_ref[...] = s0_ref[0]

    r = _iota((blk, blk), 0)
    c = _iota((blk, blk), 1)
    tril = jnp.logical_and(_div_pow2(r, CHUNK) == _div_pow2(c, CHUNK), r >= c)
    lane_lo = c < CHUNK
    row_lo = r < CHUNK
    dot = functools.partial(jnp.dot, preferred_element_type=F32)

    def block(ref, b0, cols, nchunk):
        return ref[b0:b0 + blk, cols] if nchunk == 2 else _pad_rows(ref[b0:b0 + CHUNK, cols], blk)

    for b0 in range(0, rows, blk):
        nchunk = min(blk, rows - b0) // CHUNK
        for hp in range(pairs):
            cols = slice(hp * LANES, (hp + 1) * LANES)
            qe = block(qe_ref, b0, cols, nchunk)
            keb = block(ke_ref, b0, cols, nchunk)
            kt_t = block(kt_ref, b0, cols, nchunk).T
            dec_t = block(dec_ref, b0, cols, nchunk).T
            dec_sw = pltpu.roll(dec_t, CHUNK, axis=1)
            decay = (jnp.where(lane_lo, dec_t, dec_sw), jnp.where(lane_lo, dec_sw, dec_t))
            vbs, qms, intra = [], [], []
            for hh in range(2):
                h = 2 * hp + hh
                vb = block(v_ref, b0, slice(h * LANES, (h + 1) * LANES), nchunk)
                qm = jnp.where(lane_lo if hh == 0 else jnp.logical_not(lane_lo), qe, jnp.zeros_like(qe))
                a = jnp.where(tril, _dot_nt(qm, keb), 0.0)
                intra.append(dot(a.astype(BF16), vb))
                vbs.append(vb)
                qms.append(qm)
            state = s_ref[hp]
            outs = [[], []]
            for ch in range(nchunk):
                sb = state.astype(BF16)
                rs = slice(ch * CHUNK, (ch + 1) * CHUNK)
                for hh in range(2):
                    outs[hh].append(dot(qms[hh][rs], sb) + intra[hh][rs])
                kt_c = jnp.where(lane_lo if ch == 0 else jnp.logical_not(lane_lo), kt_t, 0.0).astype(BF16)
                upd = jnp.where(row_lo, dot(kt_c, vbs[0]), dot(kt_c, vbs[1]))
                state = decay[ch] * state + upd
            s_ref[hp] = state
            for hh in range(2):
                h = 2 * hp + hh
                res = outs[hh][0] if nchunk == 1 else jnp.concatenate(outs[hh], axis=0)
                o_ref[b0:b0 + nchunk * CHUNK, h * LANES:(h + 1) * LANES] = res

    @pl.when(t == pl.num_programs(1) - 1)
    def _():
        sf_ref[0] = s_ref[...]


def _pad_rows(a, rows):
    return jnp.concatenate([a, jnp.zeros((rows - a.shape[0], a.shape[1]), a.dtype)], axis=0)


def _gla(qe, ke, kt, dec, v, s0, batch, seq):
    tg = _row_tile(seq, GLA_ROWS)
    nt = seq // tg
    pairs = GLA_HEADS // 2
    dk = GLA_HEADS * GLA_DK
    row = lambda width: pl.BlockSpec((tg, width), lambda b, t: (b * nt + t, 0))
    state = pl.BlockSpec((1, pairs, LANES, LANES), lambda b, t: (b, 0, 0, 0))
    return pl.pallas_call(
        _gla_kernel,
        out_shape=(jax.ShapeDtypeStruct((batch * seq, GLA_W), F32),
                   jax.ShapeDtypeStruct((batch, pairs, LANES, LANES), F32)),
        grid=(batch, nt),
        in_specs=[row(dk), row(dk), row(dk), row(dk), row(GLA_W), state],
        out_specs=(row(GLA_W), state),
        scratch_shapes=[pltpu.VMEM((pairs, LANES, LANES), F32)],
        compiler_params=_cparams(("arbitrary", "arbitrary")),
        name="gla_recurrence",
    )(qe, ke, kt, dec, v, s0)


def _merge_kernel(x_ref, of_ref, fg_ref, om_ref, mg_ref, og_ref, gg_ref, gn_ref, w_ref, lg_ref, lb_ref,
                  o_ref, y_ref, *, alpha):
    for h in range(FOX_HEADS):
        cols = slice(h * LANES, (h + 1) * LANES)
        y_ref[:, cols] = (of_ref[h].astype(F32) * _silu(fg_ref[:, cols].astype(F32))).astype(BF16)
    for h in range(MLA_HEADS):
        cols = slice(h * LANES, (h + 1) * LANES)
        y_ref[:, FOX_W + h * LANES:FOX_W + (h + 1) * LANES] = (
            om_ref[h].astype(F32) * _silu(mg_ref[:, cols].astype(F32))).astype(BF16)
    for h in range(GLA_HEADS):
        cols = slice(h * GLA_DV, (h + 1) * GLA_DV)
        og = og_ref[:, cols]
        og = og * lax.rsqrt(jnp.mean(og * og, axis=-1, keepdims=True) + RMS_EPS) * gn_ref[...]
        base = FOX_W + MLA_W + h * GLA_DV
        y_ref[:, base:base + GLA_DV] = (og * _silu(gg_ref[:, cols].astype(F32))).astype(BF16)
    z = alpha * x_ref[...] + jnp.dot(y_ref[...], w_ref[...], preferred_element_type=F32)
    mu = jnp.mean(z, axis=-1, keepdims=True)
    zc = z - mu
    var = jnp.mean(zc * zc, axis=-1, keepdims=True)
    o_ref[...] = zc * lax.rsqrt(var + LN_EPS) * lg_ref[...] + lb_ref[...]


def _merge(x2, o_fox, fg, o_mla, mg, o_gla, gg, gnorm, w_out, ln_g, ln_b, alpha, layer):
    rows, d = x2.shape
    tm = _row_tile(rows, MERGE_ROWS)
    row = lambda width: pl.BlockSpec((tm, width), lambda i: (i, 0))
    head = lambda n: pl.BlockSpec((n, tm, LANES), lambda i: (0, i, 0))
    const = lambda a: _layer_spec(a, layer)
    return pl.pallas_call(
        functools.partial(_merge_kernel, alpha=alpha),
        out_shape=jax.ShapeDtypeStruct((rows, d), F32),
        grid=(rows // tm,),
        in_specs=[row(d), head(FOX_HEADS), row(FOX_W), head(MLA_HEADS), row(MLA_W), row(GLA_W), row(GLA_W),
                  const(gnorm), const(w_out), const(ln_g), const(ln_b)],
        out_specs=row(d),
        scratch_shapes=[pltpu.VMEM((tm, d), BF16)],
        compiler_params=_cparams(("arbitrary",)),
        name="merge",
    )(x2, o_fox, fg, o_mla, mg, o_gla, gg, gnorm, w_out, ln_g, ln_b)


def _past_cumsum_kernel(lf_ref, o_ref):
    n, past = lf_ref.shape
    tl = min(past, 512)
    r = _iota((tl, tl), 0)
    c = _iota((tl, tl), 1)
    triu = jnp.where(r <= c, 1.0, 0.0).astype(BF16)
    d = functools.partial(jnp.dot, preferred_element_type=F32)
    carry = jnp.zeros((n, 1), F32)
    for j in range(0, past, tl):
        a1, a2, a3 = _split3(lf_ref[:, j:j + tl])
        cum = d(a1, triu) + d(a2, triu) + d(a3, triu) + carry
        o_ref[:, j:j + tl] = cum
        carry = cum[:, tl - 1:tl]
    o_ref[...] = (o_ref[...] - carry) * LOG2E


def _past_cumsum(lf_rows):
    n, past = lf_rows.shape
    tn = _row_tile(n, PROJ_ROWS)
    spec = pl.BlockSpec((tn, past), lambda i: (i, 0))
    return pl.pallas_call(
        _past_cumsum_kernel,
        out_shape=jax.ShapeDtypeStruct(lf_rows.shape, F32),
        grid=(n // tn,),
        in_specs=[spec],
        out_specs=spec,
        compiler_params=_cparams(("arbitrary",)),
        name="past_cumsum",
    )(lf_rows)


def _sample_softmax_step(s, v, m_ref, l_ref, acc_ref, h):
    m_prev = m_ref[h]
    m_next = jnp.maximum(m_prev, jnp.max(s, axis=1, keepdims=True))
    alpha = jnp.exp2(m_prev - m_next)
    p = jnp.exp2(s - m_next[:, :1])
    l_ref[h] = alpha * l_ref[h] + jnp.sum(p, axis=1, keepdims=True)
    acc_ref[h] = alpha * acc_ref[h] + jnp.dot(p.astype(BF16), v, preferred_element_type=F32)
    m_ref[h] = m_next


def _fox_sample_kernel(q_ref, ck_ref, cv_ref, cp_ref, kn_ref, vn_ref, cc_ref, cr_ref,
                       o_ref, m_ref, l_ref, acc_ref):
    j = pl.program_id(1)
    s_new = q_ref.shape[1]

    @pl.when(j == 0)
    def _():
        m_ref[...] = jnp.full_like(m_ref, -jnp.inf)
        l_ref[...] = jnp.zeros_like(l_ref)
        acc_ref[...] = jnp.zeros_like(acc_ref)

    for h in range(FOX_HEADS):
        s = _dot_nt(q_ref[h, :, :LANES], ck_ref[0, 0, h].astype(BF16))
        s = s + (cc_ref[:, h:h + 1] - cp_ref[0, h, 0])
        _sample_softmax_step(s, cv_ref[0, 0, h].astype(BF16), m_ref, l_ref, acc_ref, h)

    @pl.when(j == pl.num_programs(1) - 1)
    def _():
        keep = _iota((s_new, s_new), 1) <= _iota((s_new, s_new), 0)
        for h in range(FOX_HEADS):
            s = _dot_nt(q_ref[h, :, :LANES], kn_ref[h, :, :LANES])
            s = s + (cc_ref[:, h:h + 1] - cr_ref[0, h:h + 1, :])
            s = jnp.where(keep, s, -jnp.inf)
            _sample_softmax_step(s, vn_ref[h], m_ref, l_ref, acc_ref, h)
            o_ref[h] = (acc_ref[h] / l_ref[h]).astype(BF16)


def _fox_sample(q, cache_k, cache_v, c_past, k_new, v_new, ccol, crow, layer):
    _, batch, _, past, _ = cache_k.shape
    s_new = q.shape[1] // batch
    tk = _row_tile(past, SAMPLE_TK)
    new = lambda width: pl.BlockSpec((FOX_HEADS, s_new, width), lambda b, j: (0, b, 0))
    cache = pl.BlockSpec((1, 1, FOX_HEADS, tk, FOX_HD), lambda b, j: (layer, b, 0, j, 0))
    return pl.pallas_call(
        _fox_sample_kernel,
        out_shape=jax.ShapeDtypeStruct((FOX_HEADS, batch * s_new, FOX_HD), BF16),
        grid=(batch, past // tk),
        in_specs=[new(QK), cache, cache,
                  pl.BlockSpec((1, FOX_HEADS, 1, 1, tk), lambda b, j: (layer, 0, b, 0, j)),
                  new(QK), new(FOX_HD),
                  pl.BlockSpec((s_new, LANES), lambda b, j: (b, 0)),
                  pl.BlockSpec((1, SUBLANES, s_new), lambda b, j: (b, 0, 0))],
        out_specs=new(FOX_HD),
        scratch_shapes=[pltpu.VMEM((FOX_HEADS, s_new, LANES), F32),
                        pltpu.VMEM((FOX_HEADS, s_new, LANES), F32),
                        pltpu.VMEM((FOX_HEADS, s_new, FOX_HD), F32)],
        compiler_params=_cparams(("arbitrary", "arbitrary")),
        name="fox_sample_attn",
    )(q, cache_k, cache_v, c_past, k_new, v_new, ccol, crow)


def _mla_sample_kernel(q_ref, wuk_ref, wuv_ref, cc_ref, cr_ref, cn_ref, rn_ref,
                       o_ref, ql_ref, qr_ref, m_ref, l_ref, acc_ref):
    j = pl.program_id(1)
    s_new = q_ref.shape[1]
    dot = functools.partial(jnp.dot, preferred_element_type=F32)

    @pl.when(j == 0)
    def _():
        m_ref[...] = jnp.full_like(m_ref, -jnp.inf)
        l_ref[...] = jnp.zeros_like(l_ref)
        acc_ref[...] = jnp.zeros_like(acc_ref)
        for h in range(MLA_HEADS):
            rs = slice(h * s_new, (h + 1) * s_new)
            ql_ref[rs, :] = dot(q_ref[h, :, :LANES], wuk_ref[h]).astype(BF16)
            qr_ref[rs, :] = q_ref[h, :, LANES:]

    def update(c_keys, s_rope):
        s = _dot_nt(ql_ref[...], c_keys) + s_rope
        m_prev = m_ref[...]
        m_next = jnp.maximum(m_prev, jnp.max(s, axis=1, keepdims=True))
        alpha = jnp.exp2(m_prev - m_next)
        p = jnp.exp2(s - m_next[:, :1])
        l_ref[...] = alpha * l_ref[...] + jnp.sum(p, axis=1, keepdims=True)
        acc_ref[...] = alpha[:, :1] * acc_ref[...] + dot(p.astype(BF16), c_keys)
        m_ref[...] = m_next

    r_t = cr_ref[0, 0].astype(BF16)
    update(cc_ref[0, 0].astype(BF16), dot(qr_ref[...], jnp.concatenate([r_t, jnp.zeros_like(r_t)], axis=0)))

    @pl.when(j == pl.num_programs(1) - 1)
    def _():
        rn = jnp.where(_iota((s_new, LANES), 1) < MLA_ROPE, rn_ref[...], 0.0).astype(BF16)
        update(cn_ref[...].astype(BF16), _dot_nt(qr_ref[...], rn))
        for h in range(MLA_HEADS):
            rs = slice(h * s_new, (h + 1) * s_new)
            o_lat = (acc_ref[rs, :] / l_ref[rs, :1]).astype(BF16)
            o_ref[h] = dot(o_lat, wuv_ref[h]).astype(BF16)


def _mla_sample(q, wuk, wuv, cache_c, cache_r_t, c_new, r_new, layer):
    _, batch, past, _ = cache_c.shape
    s_new = q.shape[1] // batch
    tk = _row_tile(past, SAMPLE_TK)
    rows = MLA_HEADS * s_new
    new = lambda width: pl.BlockSpec((s_new, width), lambda b, j: (b, 0))
    head = lambda width: pl.BlockSpec((MLA_HEADS, s_new, width), lambda b, j: (0, b, 0))
    const = lambda a: _layer_spec(a, layer)
    return pl.pallas_call(
        _mla_sample_kernel,
        out_shape=jax.ShapeDtypeStruct((MLA_HEADS, batch * s_new, MLA_V), BF16),
        grid=(batch, past // tk),
        in_specs=[head(QK), const(wuk), const(wuv),
                  pl.BlockSpec((1, 1, tk, MLA_KV_RANK), lambda b, j: (layer, b, j, 0)),
                  pl.BlockSpec((1, 1, MLA_ROPE, tk), lambda b, j: (layer, b, 0, j)),
                  pl.BlockSpec((None, s_new, MLA_KV_RANK), lambda b, j: (layer, b, 0)), new(LANES)],
        out_specs=head(MLA_V),
        scratch_shapes=[pltpu.VMEM((rows, MLA_KV_RANK), BF16),
                        pltpu.VMEM((rows, LANES), BF16),
                        pltpu.VMEM((rows, LANES), F32),
                        pltpu.VMEM((rows, LANES), F32),
                        pltpu.VMEM((rows, MLA_KV_RANK), F32)],
        compiler_params=_cparams(("arbitrary", "arbitrary")),
        name="mla_sample_attn",
    )(q, wuk, wuv, cache_c, cache_r_t, c_new, r_new)


def _pad_cols(a, width):
    return jnp.pad(a, ((0, 0), (0, width - a.shape[1])))


W_FOX_COLS = 4 * FOX_W + LANES
W_MLA_COLS = MLA_HEADS * QK + MLA_KV_RANK + LANES + MLA_W
W_GLA_COLS = 2 * GLA_HEADS * GLA_DK + GLA_W + LANES + GLA_W
W_IN_BLOCK_COLS = -(-IN_OFFS[-1] // LANES) * LANES
PACK_ROWS = 256


def _pack_kernel(w_ref, fox_ref, mla_ref, gla_ref):
    rows = w_ref.shape[1]
    lane = _iota((rows, LANES), 1)

    def take(src, width):
        start = (src // LANES) * LANES
        shift = src - start
        span = -(-(shift + width) // LANES) * LANES
        win = w_ref[0, :, start:start + span]
        if shift:
            win = pltpu.roll(win, span - shift, axis=1)
        return win[:, :width]

    def narrow(src, width):
        return jnp.where(lane < width, take(src, LANES), 0.0)

    def rope_pair(src):
        p = take(src, LANES)
        quarter = MLA_ROPE // 2
        return jnp.where(lane < MLA_ROPE, p,
                         jnp.where(lane < MLA_ROPE + quarter, pltpu.roll(p, quarter, axis=1),
                                   pltpu.roll(p, LANES - quarter, axis=1)))

    def put(ref, col, val):
        ref[0, :, col:col + val.shape[1]] = val.astype(BF16)

    fq, fk, fv, ff, fg, mq, mc, mr, mg, gq, gk, gv, ggk, gg = IN_OFFS[:-1]
    put(fox_ref, 0, take(fq, FOX_W))
    put(fox_ref, FOX_W, take(fk, FOX_W))
    put(fox_ref, 2 * FOX_W, take(fv, FOX_W))
    put(fox_ref, 3 * FOX_W, take(fg, FOX_W))
    put(fox_ref, 4 * FOX_W, narrow(ff, FOX_HEADS))
    for h in range(MLA_HEADS):
        src = mq + h * (MLA_NOPE + MLA_ROPE)
        put(mla_ref, h * QK, take(src, MLA_NOPE))
        put(mla_ref, h * QK + LANES, rope_pair(src + MLA_NOPE))
    col = MLA_HEADS * QK
    put(mla_ref, col, take(mc, MLA_KV_RANK))
    put(mla_ref, col + MLA_KV_RANK, rope_pair(mr))
    put(mla_ref, col + MLA_KV_RANK + LANES, take(mg, MLA_W))
    dk = GLA_HEADS * GLA_DK
    put(gla_ref, 0, take(gq, dk))
    put(gla_ref, dk, take(gk, dk))
    put(gla_ref, 2 * dk, take(gv, GLA_W))
    put(gla_ref, 2 * dk + GLA_W, narrow(ggk, GLA_GATE_RANK))
    put(gla_ref, 2 * dk + GLA_W + LANES, take(gg, GLA_W))


def _pack_w_in(w_in):
    depth, d, _ = w_in.shape
    tr = _row_tile(d, PACK_ROWS)
    out = lambda cols: pl.BlockSpec((1, tr, cols), lambda l, i: (l, i, 0))
    return pl.pallas_call(
        _pack_kernel,
        out_shape=(jax.ShapeDtypeStruct((depth, d, W_FOX_COLS), BF16),
                   jax.ShapeDtypeStruct((depth, d, W_MLA_COLS), BF16),
                   jax.ShapeDtypeStruct((depth, d, W_GLA_COLS), BF16)),
        grid=(depth, d // tr),
        in_specs=[pl.BlockSpec((1, tr, W_IN_BLOCK_COLS), lambda l, i: (l, i, 0))],
        out_specs=(out(W_FOX_COLS), out(W_MLA_COLS), out(W_GLA_COLS)),
        compiler_params=_cparams(("arbitrary", "arbitrary")),
        name="pack_w_in",
    )(w_in)


def _stacked_params(w_in, b_f, kv_norm, w_up, w_gk_up, b_gk, gla_norm, w_out, ln_g, ln_b):
    depth = w_in.shape[0]
    w_fox, w_mla, w_gla = _pack_w_in(w_in)
    w_up_k = w_up[:, :, :, :MLA_NOPE]
    w_up_v = w_up[:, :, :, MLA_NOPE:]
    return dict(
        w_fox=w_fox,
        b_f=jnp.pad(b_f, ((0, 0), (0, LANES - FOX_HEADS)))[:, None, :],
        w_mla=w_mla,
        kv_norm=kv_norm[:, None, :],
        w_up=jnp.concatenate([w_up_k.reshape(depth, MLA_KV_RANK, MLA_W),
                              w_up_v.reshape(depth, MLA_KV_RANK, MLA_W)], axis=2).astype(BF16),
        w_uk=jnp.transpose(w_up_k, (0, 2, 3, 1)).astype(BF16),
        w_uv=jnp.transpose(w_up_v, (0, 2, 1, 3)).astype(BF16),
        w_gla=w_gla,
        w_gk=jnp.pad(w_gk_up, ((0, 0), (0, LANES - GLA_GATE_RANK), (0, 0))).astype(BF16),
        b_gk=b_gk[:, None, :],
        gla_norm=gla_norm[:, None, :],
        w_out=w_out.astype(BF16),
        ln_g=ln_g[:, None, :],
        ln_b=ln_b[:, None, :],
    )


def _rope_table(pos, rows):
    half = MLA_ROPE // 2
    inv = ROPE_BASE ** (-jnp.arange(half, dtype=F32) / half)
    ang = pos.astype(F32)[:, None] * inv[None, :]
    cos, sin = jnp.cos(ang), jnp.sin(ang)
    tab = jnp.concatenate([cos, cos, -sin, sin], axis=1)
    reps = max(rows // tab.shape[0], 1)
    return jnp.tile(tab, (reps, 1))


def _branches(x2, p, sel, tab, batch, seq, with_kv, layer, prev):
    fox = _fox_proj(x2, p["w_fox"], p["b_f"], sel, batch, seq, layer, prev and prev[:3])
    mla = _mla_proj(x2, p["w_mla"], tab, p["kv_norm"], p["w_up"], with_kv, layer, prev and prev[3:])
    gla = _gla_proj(x2, p["w_gla"], p["w_gk"], p["b_gk"], layer)
    return fox, mla, gla


def kernel(x_prompt, x_sample, cache_fox_k, cache_fox_v, cache_fox_logf, cache_mla_ckv, cache_mla_krope,
           state_gla, w_in, b_fox_f, mla_kv_norm, w_mla_kv_up, w_gla_gk_up, b_gla_gk, gla_norm, w_out,
           ln_g, ln_b):
    depth = w_in.shape[0]
    bp, tp, d = x_prompt.shape
    bs, ts, _ = x_sample.shape
    past = cache_fox_k.shape[2]
    alpha = (2 * depth) ** 0.25
    pairs = GLA_HEADS // 2

    tab_p = _rope_table(jnp.arange(tp), _row_tile(bp * tp, PROJ_ROWS))
    tab_s = _rope_table(past + jnp.arange(ts), _row_tile(bs * ts, PROJ_ROWS))
    sel = _fox_bias_selectors()

    cache_k = jnp.transpose(cache_fox_k, (0, 1, 3, 2, 4))
    cache_v = jnp.transpose(cache_fox_v, (0, 1, 3, 2, 4))
    lf_rows = jnp.transpose(cache_fox_logf.astype(F32), (0, 3, 1, 2))
    c_past = _past_cumsum(lf_rows.reshape(depth * FOX_HEADS * bs, past)).reshape(depth, FOX_HEADS, bs, 1, past)
    cache_r_t = jnp.swapaxes(cache_mla_krope, 2, 3)

    xp = x_prompt.reshape(bp * tp, d)
    xs = x_sample.reshape(bs * ts, d)
    zero_state = jnp.zeros((bp, pairs, LANES, LANES), F32)
    state_s = state_gla.reshape(depth, bs, pairs, LANES, LANES)
    p = _stacked_params(w_in, b_fox_f, mla_kv_norm, w_mla_kv_up, w_gla_gk_up, b_gla_gk, gla_norm, w_out,
                        ln_g, ln_b)
    merge_args = (p["gla_norm"], p["w_out"], p["ln_g"], p["ln_b"], alpha)
    kept_p = kept_s = None
    gla_p, gla_s = [], []

    for l in range(depth):
        (fk, fv, flf, fq, fkc, fvb, fg, _, _), (mc, mkr, mq, _, mg, mkc, mv), (*gla_in, gg) = \
            _branches(xp, p, sel, tab_p, bp, tp, True, l, kept_p)
        kept_p = (fk, fv, flf, mc, mkr)
        o_f = _prompt_attn(fq, fkc, fvb, bp, tp, chunked=False)
        o_m = _prompt_attn(mq, mkc, mv, bp, tp, chunked=True)
        o_g, s_p = _gla(*gla_in, zero_state, bp, tp)
        gla_p.append(s_p.reshape(bp, GLA_HEADS, GLA_DK, GLA_DV))
        xp = _merge(xp, o_f, fg, o_m, mg, o_g, gg, *merge_args, l)

        (fk, fv, flf, fq, fkc, fvb, fg, fcc, fcr), (mc, mkr, mq, mkr2, mg), (*gla_in, gg) = \
            _branches(xs, p, sel, tab_s, bs, ts, False, l, kept_s)
        kept_s = (fk, fv, flf, mc, mkr)
        crow_s = jnp.swapaxes(fcr.reshape(SUBLANES, bs, ts), 0, 1)
        o_f = _fox_sample(fq, cache_k, cache_v, c_past, fkc, fvb, fcc, crow_s, l)
        o_m = _mla_sample(mq, p["w_uk"], p["w_uv"], cache_mla_ckv, cache_r_t, mc, mkr2, l)
        o_g, s_s = _gla(*gla_in, state_s[l], bs, ts)
        gla_s.append(s_s.reshape(bs, GLA_HEADS, GLA_DK, GLA_DV))
        xs = _merge(xs, o_f, fg, o_m, mg, o_g, gg, *merge_args, l)

    def returned(kept, states, batch, seq):
        fk, fv, flf, mc, mkr = kept
        return (jnp.transpose(fk, (0, 1, 3, 2, 4)), jnp.transpose(fv, (0, 1, 3, 2, 4)),
                flf[:, :, :FOX_HEADS].reshape(depth, batch, seq, FOX_HEADS),
                mc.reshape(depth, batch, seq, MLA_KV_RANK), mkr.reshape(depth, batch, seq, MLA_ROPE),
                jnp.stack(states))

    return ((xp.reshape(bp, tp, d), xs.reshape(bs, ts, d))
            + returned(kept_p, gla_p, bp, tp) + returned(kept_s, gla_s, bs, ts))
```

```python
import functools
import math

import numpy as np
import jax
import jax.numpy as jnp
from jax import lax
from jax.experimental import pallas as pl
from jax.experimental.pallas import tpu as pltpu

F32 = jnp.float32
BF16 = jnp.bfloat16

LANES = 128
SUBLANES = 8
VMEM_LIMIT_BYTES = 60 * 1024 * 1024

CHUNK = 64
FOX_HEADS, FOX_HD = 6, 128
MLA_HEADS, MLA_NOPE, MLA_ROPE, MLA_V, MLA_KV_RANK = 6, 128, 64, 128, 512
GLA_HEADS, GLA_DK, GLA_DV, GLA_GATE_RANK = 4, 64, 128, 16
GLA_GATE_NORM = 16.0
FOX_W = FOX_HEADS * FOX_HD
MLA_W = MLA_HEADS * MLA_V
GLA_W = GLA_HEADS * GLA_DV
QK = 2 * LANES
ROPE_BASE = 10000.0
LN_EPS = 1e-5
RMS_EPS = 1e-6
LOG2E = math.log2(math.e)
FOX_SCALE = FOX_HD ** -0.5
MLA_SCALE = (MLA_NOPE + MLA_ROPE) ** -0.5
GLA_SCALE = GLA_DK ** -0.5

IN_SIZES = (FOX_W, FOX_W, FOX_W, FOX_HEADS, FOX_W,
            MLA_HEADS * (MLA_NOPE + MLA_ROPE), MLA_KV_RANK, MLA_ROPE, MLA_W,
            GLA_HEADS * GLA_DK, GLA_HEADS * GLA_DK, GLA_W, GLA_GATE_RANK, GLA_W)
IN_OFFS = tuple(int(v) for v in np.concatenate([[0], np.cumsum(IN_SIZES)]))

PROJ_ROWS = 512
MERGE_ROWS = 512
ATTN_TQ = 1024
ATTN_TK = 1024
GLA_ROWS = 256
SAMPLE_TK = 2048


def _cparams(sem):
    return pltpu.CompilerParams(dimension_semantics=sem, vmem_limit_bytes=VMEM_LIMIT_BYTES)


def _row_tile(rows, want):
    t = min(rows, want)
    assert rows % t == 0, (rows, t)
    return t


def _split3(a):
    a1 = a.astype(BF16)
    r1 = a - a1.astype(F32)
    a2 = r1.astype(BF16)
    a3 = (r1 - a2.astype(F32)).astype(BF16)
    return a1, a2, a3


def _mask_dot(mask_bf16, a):
    a1, a2, a3 = _split3(a)
    d = functools.partial(jnp.dot, preferred_element_type=F32)
    return d(mask_bf16, a1) + d(mask_bf16, a2) + d(mask_bf16, a3)


def _dot_nt(a, b):
    return lax.dot_general(a, b, (((1,), (1,)), ((), ())), preferred_element_type=F32)


def _log_sigmoid(z):
    return jnp.minimum(z, 0.0) - jnp.log1p(jnp.exp(-jnp.abs(z)))


def _silu(z):
    return z / (1.0 + jnp.exp(-z))


def _iota(shape, dim):
    return lax.broadcasted_iota(jnp.int32, shape, dim)


def _div_pow2(x, n):
    assert n > 0 and n & (n - 1) == 0, n
    return lax.shift_right_logical(x, jnp.int32(n.bit_length() - 1))


def _layer_spec(a, layer):
    zeros = (0,) * (a.ndim - 1)
    return pl.BlockSpec((None,) + a.shape[1:], lambda *_: (layer,) + zeros)


def _stacked_outputs(prev, n_inputs, first_out):
    specs = [pl.BlockSpec(memory_space=pl.ANY)] * len(prev)
    return list(prev), specs, {n_inputs + j: first_out + j for j in range(len(prev))}


def _stacked_zeros(depth, batch, seq):
    rows = batch * seq
    return (jnp.zeros((depth, batch, FOX_HEADS, seq, FOX_HD), F32),
            jnp.zeros((depth, batch, FOX_HEADS, seq, FOX_HD), F32),
            jnp.zeros((depth, rows, LANES), F32),
            jnp.zeros((depth, rows, MLA_KV_RANK), F32),
            jnp.zeros((depth, rows, MLA_ROPE), F32))


def _lane_tile(a, width):
    return jnp.concatenate([a] * (width // LANES), axis=1)


def _fox_bias_selectors():
    sq = np.zeros((3 * LANES, FOX_W), np.float32)
    sk = np.zeros((3 * LANES, FOX_W), np.float32)
    oq = np.zeros((1, FOX_W), np.float32)
    ok = np.zeros((1, FOX_W), np.float32)
    for h in range(FOX_HEADS):
        for piece in range(3):
            sq[piece * LANES + h, h * LANES + piece] = 1.0
            sk[piece * LANES + h, h * LANES + 3 + piece] = -1.0
            oq[0, h * LANES + 3 + piece] = 1.0
            ok[0, h * LANES + piece] = 1.0
    return jnp.asarray(sq, BF16), jnp.asarray(sk, BF16), jnp.asarray(oq), jnp.asarray(ok)


def _fox_proj_kernel(x_ref, w_ref, bf_ref, sq_ref, sk_ref, oq_ref, ok_ref, *refs, seq, tiles_per_seq):
    kf_ref, vf_ref, lf_ref, q_ref, k_ref, v_ref, g_ref, cc_ref, cr_ref, carry_ref = refs[-10:]
    tm = x_ref.shape[0]
    nb, _, tt, _ = kf_ref.shape
    xb = x_ref[...].astype(BF16)
    dot = functools.partial(jnp.dot, preferred_element_type=F32)

    def proj(a, b):
        return dot(xb, w_ref[:, a:b])

    q = proj(0, FOX_W) * (FOX_SCALE * LOG2E)
    k = proj(FOX_W, 2 * FOX_W)
    v = proj(2 * FOX_W, 3 * FOX_W)
    g_ref[...] = proj(3 * FOX_W, 4 * FOX_W).astype(BF16)
    lf = _log_sigmoid(proj(4 * FOX_W, 4 * FOX_W + LANES) + bf_ref[...])
    lf_ref[...] = lf

    r = _iota((tm, tm), 0)
    c = _iota((tm, tm), 1)
    tri = r >= c
    if seq < tm:
        tri = jnp.logical_and(tri, _div_pow2(r, seq) == _div_pow2(c, seq))
    c_tile = _mask_dot(jnp.where(tri, 1.0, 0.0).astype(BF16), lf)
    if seq > tm:
        @pl.when(pl.program_id(0) % tiles_per_seq == 0)
        def _():
            carry_ref[...] = jnp.zeros_like(carry_ref)
        c_tile = c_tile + carry_ref[...]
        carry_ref[...] = c_tile[tm - 1:tm, :]
    c2 = c_tile * LOG2E
    cc_ref[...] = c2
    cr_ref[...] = c2.T[:SUBLANES, :]

    pieces = jnp.concatenate(_split3(c2), axis=1)
    qx = dot(pieces, sq_ref[...]) + oq_ref[...]
    kx = dot(pieces, sk_ref[...]) + ok_ref[...]
    for h in range(FOX_HEADS):
        cols = slice(h * LANES, (h + 1) * LANES)
        q_ref[h, :, :LANES] = q[:, cols].astype(BF16)
        q_ref[h, :, LANES:] = qx[:, cols].astype(BF16)
        k_ref[h, :, :LANES] = k[:, cols].astype(BF16)
        k_ref[h, :, LANES:] = kx[:, cols].astype(BF16)
        v_ref[h] = v[:, cols].astype(BF16)
        for b in range(nb):
            kf_ref[b, h] = k[b * tt:(b + 1) * tt, cols]
            vf_ref[b, h] = v[b * tt:(b + 1) * tt, cols]


def _fox_proj(x2, w, bf, sel, batch, seq, layer, prev):
    depth = w.shape[0]
    rows, d = x2.shape
    tm = _row_tile(rows, PROJ_ROWS)
    assert seq % tm == 0 or tm % seq == 0
    tiles_per_seq = max(seq // tm, 1)
    nb, tt = max(tm // seq, 1), min(tm, seq)
    row = lambda width: pl.BlockSpec((tm, width), lambda i: (i, 0))
    head = lambda width: pl.BlockSpec((FOX_HEADS, tm, width), lambda i: (0, i, 0))
    const = lambda a: pl.BlockSpec(a.shape, lambda i: (0, 0))
    cache = pl.BlockSpec((None, nb, FOX_HEADS, tt, FOX_HD),
                         lambda i: (layer, i // tiles_per_seq, 0, i % tiles_per_seq, 0))
    out_shape = (
        jax.ShapeDtypeStruct((depth, batch, FOX_HEADS, seq, FOX_HD), F32),
        jax.ShapeDtypeStruct((depth, batch, FOX_HEADS, seq, FOX_HD), F32),
        jax.ShapeDtypeStruct((depth, rows, LANES), F32),
        jax.ShapeDtypeStruct((FOX_HEADS, rows, QK), BF16),
        jax.ShapeDtypeStruct((FOX_HEADS, rows, QK), BF16),
        jax.ShapeDtypeStruct((FOX_HEADS, rows, FOX_HD), BF16),
        jax.ShapeDtypeStruct((rows, FOX_W), BF16),
        jax.ShapeDtypeStruct((rows, LANES), F32),
        jax.ShapeDtypeStruct((SUBLANES, rows), F32),
    )
    out_specs = (cache, cache, pl.BlockSpec((None, tm, LANES), lambda i: (layer, i, 0)),
                 head(QK), head(QK), head(FOX_HD), row(FOX_W), row(LANES),
                 pl.BlockSpec((SUBLANES, tm), lambda i: (0, i)))
    in_specs = [row(d), _layer_spec(w, layer), _layer_spec(bf, layer)] + [const(a) for a in sel]
    prev_args, prev_specs, aliases = _stacked_outputs(prev, len(in_specs), 0)
    return pl.pallas_call(
        functools.partial(_fox_proj_kernel, seq=seq, tiles_per_seq=tiles_per_seq),
        out_shape=out_shape,
        grid=(rows // tm,),
        in_specs=in_specs + prev_specs,
        out_specs=out_specs,
        scratch_shapes=[pltpu.VMEM((1, LANES), F32)],
        input_output_aliases=aliases,
        compiler_params=_cparams(("arbitrary",)),
        name="fox_proj",
    )(x2, w, bf, *sel, *prev_args)


def _rope_block(blk, tab):
    t = blk * tab
    return t + pltpu.roll(t, LANES // 2, axis=1)


def _mla_proj_kernel(*refs, with_kv, n_prev):
    if with_kv:
        x_ref, w_ref, tab_ref, kvn_ref, wup_ref = refs[:5]
        c_ref, kr_ref, q_ref, kr2_ref, g_ref, kc_ref, v_ref = refs[5 + n_prev:]
    else:
        x_ref, w_ref, tab_ref, kvn_ref = refs[:4]
        c_ref, kr_ref, q_ref, kr2_ref, g_ref = refs[4 + n_prev:]
    tm = x_ref.shape[0]
    xb = x_ref[...].astype(BF16)
    tab = tab_ref[...]
    dot = functools.partial(jnp.dot, preferred_element_type=F32)

    def proj(a, b):
        return dot(xb, w_ref[:, a:b])

    for h in range(MLA_HEADS):
        hq = proj(h * QK, (h + 1) * QK) * (MLA_SCALE * LOG2E)
        q_ref[h, :, :LANES] = hq[:, :LANES].astype(BF16)
        q_ref[h, :, LANES:] = _rope_block(hq[:, LANES:], tab).astype(BF16)
    off = MLA_HEADS * QK
    mc = proj(off, off + MLA_KV_RANK)
    c = mc * lax.rsqrt(jnp.mean(mc * mc, axis=-1, keepdims=True) + RMS_EPS) * kvn_ref[...]
    c_ref[...] = c
    off += MLA_KV_RANK
    kr = _rope_block(proj(off, off + LANES), tab)
    kr_ref[...] = kr[:, :MLA_ROPE]
    kr2_ref[...] = kr
    off += LANES
    g_ref[...] = proj(off, off + MLA_W).astype(BF16)
    if with_kv:
        cb = c.astype(BF16)
        krz = jnp.where(_iota((tm, LANES), 1) < MLA_ROPE, kr, 0.0).astype(BF16)
        for h in range(MLA_HEADS):
            kc_ref[h, :, :LANES] = dot(cb, wup_ref[:, h * LANES:(h + 1) * LANES]).astype(BF16)
            kc_ref[h, :, LANES:] = krz
            v_ref[h] = dot(cb, wup_ref[:, MLA_W + h * LANES:MLA_W + (h + 1) * LANES]).astype(BF16)


def _mla_proj(x2, w, tab, kvn, wup, with_kv, layer, prev):
    depth = w.shape[0]
    rows, d = x2.shape
    tm = _row_tile(rows, PROJ_ROWS)
    tab_tiles = tab.shape[0] // tm
    row = lambda width: pl.BlockSpec((tm, width), lambda i: (i, 0))
    head = lambda width: pl.BlockSpec((MLA_HEADS, tm, width), lambda i: (0, i, 0))
    stacked = lambda width: pl.BlockSpec((None, tm, width), lambda i: (layer, i, 0))
    in_specs = [row(d), _layer_spec(w, layer),
                pl.BlockSpec((tm, LANES), lambda i: (i % tab_tiles, 0)),
                _layer_spec(kvn, layer)]
    args = [x2, w, tab, kvn]
    out_shape = [jax.ShapeDtypeStruct((depth, rows, MLA_KV_RANK), F32),
                 jax.ShapeDtypeStruct((depth, rows, MLA_ROPE), F32),
                 jax.ShapeDtypeStruct((MLA_HEADS, rows, QK), BF16),
                 jax.ShapeDtypeStruct((rows, LANES), F32),
                 jax.ShapeDtypeStruct((rows, MLA_W), BF16)]
    out_specs = [stacked(MLA_KV_RANK), stacked(MLA_ROPE), head(QK), row(LANES), row(MLA_W)]
    if with_kv:
        in_specs.append(_layer_spec(wup, layer))
        args.append(wup)
        out_shape += [jax.ShapeDtypeStruct((MLA_HEADS, rows, QK), BF16),
                      jax.ShapeDtypeStruct((MLA_HEADS, rows, MLA_V), BF16)]
        out_specs += [head(QK), head(MLA_V)]
    prev_args, prev_specs, aliases = _stacked_outputs(prev, len(in_specs), 0)
    return pl.pallas_call(
        functools.partial(_mla_proj_kernel, with_kv=with_kv, n_prev=len(prev_args)),
        out_shape=tuple(out_shape),
        grid=(rows // tm,),
        in_specs=in_specs + prev_specs,
        out_specs=tuple(out_specs),
        input_output_aliases=aliases,
        compiler_params=_cparams(("arbitrary",)),
        name="mla_proj_kv" if with_kv else "mla_proj",
    )(*args, *prev_args)


def _gla_proj_kernel(x_ref, w_ref, wgk_ref, bgk_ref, qe_ref, ke_ref, kt_ref, dec_ref, v_ref, g_ref):
    tm = x_ref.shape[0]
    xb = x_ref[...].astype(BF16)

    def proj(a, b):
        return jnp.dot(xb, w_ref[:, a:b], preferred_element_type=F32)

    dk = GLA_HEADS * GLA_DK
    q = proj(0, dk)
    k = proj(dk, 2 * dk)
    v_ref[...] = proj(2 * dk, 2 * dk + GLA_W).astype(BF16)
    off = 2 * dk + GLA_W
    low = proj(off, off + LANES).astype(BF16)
    z = jnp.dot(low, wgk_ref[...], preferred_element_type=F32) + bgk_ref[...]
    gate = _log_sigmoid(z) * (1.0 / GLA_GATE_NORM)
    off += LANES
    g_ref[...] = proj(off, off + GLA_W).astype(BF16)

    r = _iota((tm, tm), 0)
    c = _iota((tm, tm), 1)
    tril = jnp.logical_and(_div_pow2(r, CHUNK) == _div_pow2(c, CHUNK), r >= c)
    bcum = _mask_dot(jnp.where(tril, 1.0, 0.0).astype(BF16), gate)
    btot = jnp.concatenate(
        [jnp.broadcast_to(bcum[e - 1:e, :], (CHUNK, dk)) for e in range(CHUNK, tm + 1, CHUNK)], axis=0)
    qe_ref[...] = (q * jnp.exp(bcum) * GLA_SCALE).astype(BF16)
    ke_ref[...] = (k * jnp.exp(-bcum)).astype(BF16)
    kt_ref[...] = k * jnp.exp(btot - bcum)
    dec_ref[...] = jnp.exp(btot)


def _gla_proj(x2, w, wgk, bgk, layer):
    rows, d = x2.shape
    tm = _row_tile(rows, PROJ_ROWS)
    dk = GLA_HEADS * GLA_DK
    row = lambda width: pl.BlockSpec((tm, width), lambda i: (i, 0))
    const = lambda a: _layer_spec(a, layer)
    return pl.pallas_call(
        _gla_proj_kernel,
        out_shape=(jax.ShapeDtypeStruct((rows, dk), BF16),
                   jax.ShapeDtypeStruct((rows, dk), BF16),
                   jax.ShapeDtypeStruct((rows, dk), F32),
                   jax.ShapeDtypeStruct((rows, dk), F32),
                   jax.ShapeDtypeStruct((rows, GLA_W), BF16),
                   jax.ShapeDtypeStruct((rows, GLA_W), BF16)),
        grid=(rows // tm,),
        in_specs=[row(d), const(w), const(wgk), const(bgk)],
        out_specs=(row(dk), row(dk), row(dk), row(dk), row(GLA_W), row(GLA_W)),
        compiler_params=_cparams(("arbitrary",)),
        name="gla_proj",
    )(x2, w, wgk, bgk)


def _online_softmax_update(s, v, m_prev, acc_prev):
    tk = s.shape[1]
    m_next = jnp.maximum(m_prev, jnp.max(s, axis=1, keepdims=True))
    alpha = jnp.exp2(m_prev - m_next)
    p = jnp.exp2(s - _lane_tile(m_next, tk)).astype(BF16)
    v_ones = jnp.concatenate([v, jnp.ones((tk, LANES), BF16)], axis=1)
    acc = _lane_tile(alpha, QK) * acc_prev + jnp.dot(p, v_ones, preferred_element_type=F32)
    return m_next, acc


def _prompt_attn_kernel(qi_ref, ki_ref, first_ref, last_ref, mask_ref,
                        q_ref, k_ref, v_ref, o_ref, m_ref, acc_ref, bias_ref, s_ref, *, chunked):
    s_idx = pl.program_id(1)
    heads, tq, _ = q_ref.shape
    tk = k_ref.shape[1]
    half = tq // 2
    lo, hi = slice(0, half), slice(half, tq)

    @pl.when(first_ref[s_idx] == 1)
    def _():
        m_ref[...] = jnp.full_like(m_ref, -jnp.inf)
        acc_ref[...] = jnp.zeros_like(acc_ref)

    def sweep(masked):
        keys_lo = half if (masked and tq == tk) else tk

        def scores(h, rows, slot, nkeys):
            s = _dot_nt(q_ref[h, rows, :], k_ref[h, :nkeys, :])
            if masked:
                s = s + bias_ref[rows, :nkeys]
            s_ref[slot, :, :nkeys] = s

        def softmax_pv(h, rows, slot, nkeys):
            m_next, acc = _online_softmax_update(s_ref[slot, :, :nkeys], v_ref[h, :nkeys, :],
                                                 m_ref[h, rows, :], acc_ref[h, rows, :])
            m_ref[h, rows, :] = m_next
            acc_ref[h, rows, :] = acc

        scores(0, lo, 0, keys_lo)
        for h in range(heads):
            scores(h, hi, 1, tk)
            softmax_pv(h, lo, 0, keys_lo)
            if h + 1 < heads:
                scores(h + 1, lo, 0, keys_lo)
            softmax_pv(h, hi, 1, tk)

    @pl.when(mask_ref[s_idx] == 1)
    def _():
        qpos = qi_ref[s_idx] * tq + _iota((tq, tk), 0)
        kpos = ki_ref[s_idx] * tk + _iota((tq, tk), 1)
        if chunked:
            keep = _div_pow2(kpos, CHUNK) <= _div_pow2(qpos, CHUNK)
        else:
            keep = kpos <= qpos
        bias_ref[...] = jnp.where(keep, 0.0, -jnp.inf)
        sweep(True)

    @pl.when(mask_ref[s_idx] == 0)
    def _():
        sweep(False)

    @pl.when(last_ref[s_idx] == 1)
    def _():
        for h in range(heads):
            o_ref[h] = (acc_ref[h, :, :LANES] / acc_ref[h, :, LANES:]).astype(BF16)


def _attn_schedule(seq, tq, tk, chunked):
    qi, ki, first, last, mask = [], [], [], [], []
    for a in range(seq // tq):
        q_lo, q_hi = a * tq, (a + 1) * tq - 1
        if chunked:
            q_lo, q_hi = (q_lo // CHUNK) * CHUNK, (q_hi // CHUNK) * CHUNK + CHUNK - 1
        tiles = [b for b in range(seq // tk) if b * tk <= q_hi]
        for b in tiles:
            qi.append(a)
            ki.append(b)
            first.append(int(b == tiles[0]))
            last.append(int(b == tiles[-1]))
            mask.append(int((b + 1) * tk - 1 > q_lo))
    return [jnp.asarray(np.asarray(t, np.int32)) for t in (qi, ki, first, last, mask)]


def _prompt_attn(q, k, v, batch, seq, *, chunked):
    heads = q.shape[0]
    tq = _row_tile(seq, ATTN_TQ)
    tk = _row_tile(seq, ATTN_TK)
    nq, nk = seq // tq, seq // tk
    tabs = _attn_schedule(seq, tq, tk, chunked)
    steps = int(tabs[0].shape[0])
    qmap = lambda b, s, qi, ki, *_: (0, b * nq + qi[s], 0)
    kmap = lambda b, s, qi, ki, *_: (0, b * nk + ki[s], 0)
    return pl.pallas_call(
        functools.partial(_prompt_attn_kernel, chunked=chunked),
        out_shape=jax.ShapeDtypeStruct((heads, batch * seq, LANES), BF16),
        grid_spec=pltpu.PrefetchScalarGridSpec(
            num_scalar_prefetch=5,
            grid=(batch, steps),
            in_specs=[pl.BlockSpec((heads, tq, QK), qmap),
                      pl.BlockSpec((heads, tk, QK), kmap),
                      pl.BlockSpec((heads, tk, LANES), kmap)],
            out_specs=pl.BlockSpec((heads, tq, LANES), qmap),
            scratch_shapes=[pltpu.VMEM((heads, tq, LANES), F32),
                            pltpu.VMEM((heads, tq, QK), F32),
                            pltpu.VMEM((tq, tk), F32),
                            pltpu.VMEM((2, tq // 2, tk), F32)]),
        compiler_params=_cparams(("arbitrary", "arbitrary")),
        name="mla_attn" if chunked else "fox_attn",
    )(*tabs, q, k, v)


def _gla_kernel(qe_ref, ke_ref, kt_ref, dec_ref, v_ref, s0_ref, o_ref, sf_ref, s_ref):
    t = pl.program_id(1)
    rows = qe_ref.shape[0]
    blk = 2 * CHUNK
    pairs = GLA_HEADS // 2

    @pl.when(t == 0)
    def _():
        s_ref[...] = s0_ref[0]

    r = _iota((blk, blk), 0)
    c = _iota((blk, blk), 1)
    tril = jnp.logical_and(_div_pow2(r, CHUNK) == _div_pow2(c, CHUNK), r >= c)
    lane_lo = c < CHUNK
    row_lo = r < CHUNK
    dot = functools.partial(jnp.dot, preferred_element_type=F32)

    def block(ref, b0, cols, nchunk):
        return ref[b0:b0 + blk, cols] if nchunk == 2 else _pad_rows(ref[b0:b0 + CHUNK, cols], blk)

    for b0 in range(0, rows, blk):
        nchunk = min(blk, rows - b0) // CHUNK
        for hp in range(pairs):
            cols = slice(hp * LANES, (hp + 1) * LANES)
            qe = block(qe_ref, b0, cols, nchunk)
            keb = block(ke_ref, b0, cols, nchunk)
            kt_t = block(kt_ref, b0, cols, nchunk).T
            dec_t = block(dec_ref, b0, cols, nchunk).T
            dec_sw = pltpu.roll(dec_t, CHUNK, axis=1)
            decay = (jnp.where(lane_lo, dec_t, dec_sw), jnp.where(lane_lo, dec_sw, dec_t))
            vbs, qms, intra = [], [], []
            for hh in range(2):
                h = 2 * hp + hh
                vb = block(v_ref, b0, slice(h * LANES, (h + 1) * LANES), nchunk)
                qm = jnp.where(lane_lo if hh == 0 else jnp.logical_not(lane_lo), qe, jnp.zeros_like(qe))
                a = jnp.where(tril, _dot_nt(qm, keb), 0.0)
                intra.append(dot(a.astype(BF16), vb))
                vbs.append(vb)
                qms.append(qm)
            state = s_ref[hp]
            outs = [[], []]
            for ch in range(nchunk):
                sb = state.astype(BF16)
                rs = slice(ch * CHUNK, (ch + 1) * CHUNK)
                for hh in range(2):
                    outs[hh].append(dot(qms[hh][rs], sb) + intra[hh][rs])
                kt_c = jnp.where(lane_lo if ch == 0 else jnp.logical_not(lane_lo), kt_t, 0.0).astype(BF16)
                upd = jnp.where(row_lo, dot(kt_c, vbs[0]), dot(kt_c, vbs[1]))
                state = decay[ch] * state + upd
            s_ref[hp] = state
            for hh in range(2):
                h = 2 * hp + hh
                res = outs[hh][0] if nchunk == 1 else jnp.concatenate(outs[hh], axis=0)
                o_ref[b0:b0 + nchunk * CHUNK, h * LANES:(h + 1) * LANES] = res

    @pl.when(t == pl.num_programs(1) - 1)
    def _():
        sf_ref[0] = s_ref[...]


def _pad_rows(a, rows):
    return jnp.concatenate([a, jnp.zeros((rows - a.shape[0], a.shape[1]), a.dtype)], axis=0)


def _gla(qe, ke, kt, dec, v, s0, batch, seq):
    tg = _row_tile(seq, GLA_ROWS)
    nt = seq // tg
    pairs = GLA_HEADS // 2
    dk = GLA_HEADS * GLA_DK
    row = lambda width: pl.BlockSpec((tg, width), lambda b, t: (b * nt + t, 0))
    state = pl.BlockSpec((1, pairs, LANES, LANES), lambda b, t: (b, 0, 0, 0))
    return pl.pallas_call(
        _gla_kernel,
        out_shape=(jax.ShapeDtypeStruct((batch * seq, GLA_W), F32),
                   jax.ShapeDtypeStruct((batch, pairs, LANES, LANES), F32)),
        grid=(batch, nt),
        in_specs=[row(dk), row(dk), row(dk), row(dk), row(GLA_W), state],
        out_specs=(row(GLA_W), state),
        scratch_shapes=[pltpu.VMEM((pairs, LANES, LANES), F32)],
        compiler_params=_cparams(("arbitrary", "arbitrary")),
        name="gla_recurrence",
    )(qe, ke, kt, dec, v, s0)


def _merge_kernel(x_ref, of_ref, fg_ref, om_ref, mg_ref, og_ref, gg_ref, gn_ref, w_ref, lg_ref, lb_ref,
                  o_ref, y_ref, *, alpha):
    for h in range(FOX_HEADS):
        cols = slice(h * LANES, (h + 1) * LANES)
        y_ref[:, cols] = (of_ref[h].astype(F32) * _silu(fg_ref[:, cols].astype(F32))).astype(BF16)
    for h in range(MLA_HEADS):
        cols = slice(h * LANES, (h + 1) * LANES)
        y_ref[:, FOX_W + h * LANES:FOX_W + (h + 1) * LANES] = (
            om_ref[h].astype(F32) * _silu(mg_ref[:, cols].astype(F32))).astype(BF16)
    for h in range(GLA_HEADS):
        cols = slice(h * GLA_DV, (h + 1) * GLA_DV)
        og = og_ref[:, cols]
        og = og * lax.rsqrt(jnp.mean(og * og, axis=-1, keepdims=True) + RMS_EPS) * gn_ref[...]
        base = FOX_W + MLA_W + h * GLA_DV
        y_ref[:, base:base + GLA_DV] = (og * _silu(gg_ref[:, cols].astype(F32))).astype(BF16)
    z = alpha * x_ref[...] + jnp.dot(y_ref[...], w_ref[...], preferred_element_type=F32)
    mu = jnp.mean(z, axis=-1, keepdims=True)
    zc = z - mu
    var = jnp.mean(zc * zc, axis=-1, keepdims=True)
    o_ref[...] = zc * lax.rsqrt(var + LN_EPS) * lg_ref[...] + lb_ref[...]


def _merge(x2, o_fox, fg, o_mla, mg, o_gla, gg, gnorm, w_out, ln_g, ln_b, alpha, layer):
    rows, d = x2.shape
    tm = _row_tile(rows, MERGE_ROWS)
    row = lambda width: pl.BlockSpec((tm, width), lambda i: (i, 0))
    head = lambda n: pl.BlockSpec((n, tm, LANES), lambda i: (0, i, 0))
    const = lambda a: _layer_spec(a, layer)
    return pl.pallas_call(
        functools.partial(_merge_kernel, alpha=alpha),
        out_shape=jax.ShapeDtypeStruct((rows, d), F32),
        grid=(rows // tm,),
        in_specs=[row(d), head(FOX_HEADS), row(FOX_W), head(MLA_HEADS), row(MLA_W), row(GLA_W), row(GLA_W),
                  const(gnorm), const(w_out), const(ln_g), const(ln_b)],
        out_specs=row(d),
        scratch_shapes=[pltpu.VMEM((tm, d), BF16)],
        compiler_params=_cparams(("arbitrary",)),
        name="merge",
    )(x2, o_fox, fg, o_mla, mg, o_gla, gg, gnorm, w_out, ln_g, ln_b)


def _past_cumsum_kernel(lf_ref, o_ref):
    n, past = lf_ref.shape
    tl = min(past, 512)
    r = _iota((tl, tl), 0)
    c = _iota((tl, tl), 1)
    triu = jnp.where(r <= c, 1.0, 0.0).astype(BF16)
    d = functools.partial(jnp.dot, preferred_element_type=F32)
    carry = jnp.zeros((n, 1), F32)
    for j in range(0, past, tl):
        a1, a2, a3 = _split3(lf_ref[:, j:j + tl])
        cum = d(a1, triu) + d(a2, triu) + d(a3, triu) + carry
        o_ref[:, j:j + tl] = cum
        carry = cum[:, tl - 1:tl]
    o_ref[...] = (o_ref[...] - carry) * LOG2E


def _past_cumsum(lf_rows):
    n, past = lf_rows.shape
    tn = _row_tile(n, PROJ_ROWS)
    spec = pl.BlockSpec((tn, past), lambda i: (i, 0))
    return pl.pallas_call(
        _past_cumsum_kernel,
        out_shape=jax.ShapeDtypeStruct(lf_rows.shape, F32),
        grid=(n // tn,),
        in_specs=[spec],
        out_specs=spec,
        compiler_params=_cparams(("arbitrary",)),
        name="past_cumsum",
    )(lf_rows)


def _sample_softmax_step(s, v, m_ref, l_ref, acc_ref, h):
    m_prev = m_ref[h]
    m_next = jnp.maximum(m_prev, jnp.max(s, axis=1, keepdims=True))
    alpha = jnp.exp2(m_prev - m_next)
    p = jnp.exp2(s - m_next[:, :1])
    l_ref[h] = alpha * l_ref[h] + jnp.sum(p, axis=1, keepdims=True)
    acc_ref[h] = alpha * acc_ref[h] + jnp.dot(p.astype(BF16), v, preferred_element_type=F32)
    m_ref[h] = m_next


def _fox_sample_kernel(q_ref, ck_ref, cv_ref, cp_ref, kn_ref, vn_ref, cc_ref, cr_ref,
                       o_ref, m_ref, l_ref, acc_ref):
    j = pl.program_id(1)
    s_new = q_ref.shape[1]

    @pl.when(j == 0)
    def _():
        m_ref[...] = jnp.full_like(m_ref, -jnp.inf)
        l_ref[...] = jnp.zeros_like(l_ref)
        acc_ref[...] = jnp.zeros_like(acc_ref)

    for h in range(FOX_HEADS):
        s = _dot_nt(q_ref[h, :, :LANES], ck_ref[0, 0, h].astype(BF16))
        s = s + (cc_ref[:, h:h + 1] - cp_ref[0, h, 0])
        _sample_softmax_step(s, cv_ref[0, 0, h].astype(BF16), m_ref, l_ref, acc_ref, h)

    @pl.when(j == pl.num_programs(1) - 1)
    def _():
        keep = _iota((s_new, s_new), 1) <= _iota((s_new, s_new), 0)
        for h in range(FOX_HEADS):
            s = _dot_nt(q_ref[h, :, :LANES], kn_ref[h, :, :LANES])
            s = s + (cc_ref[:, h:h + 1] - cr_ref[0, h:h + 1, :])
            s = jnp.where(keep, s, -jnp.inf)
            _sample_softmax_step(s, vn_ref[h], m_ref, l_ref, acc_ref, h)
            o_ref[h] = (acc_ref[h] / l_ref[h]).astype(BF16)


def _fox_sample(q, cache_k, cache_v, c_past, k_new, v_new, ccol, crow, layer):
    _, batch, _, past, _ = cache_k.shape
    s_new = q.shape[1] // batch
    tk = _row_tile(past, SAMPLE_TK)
    new = lambda width: pl.BlockSpec((FOX_HEADS, s_new, width), lambda b, j: (0, b, 0))
    cache = pl.BlockSpec((1, 1, FOX_HEADS, tk, FOX_HD), lambda b, j: (layer, b, 0, j, 0))
    return pl.pallas_call(
        _fox_sample_kernel,
        out_shape=jax.ShapeDtypeStruct((FOX_HEADS, batch * s_new, FOX_HD), BF16),
        grid=(batch, past // tk),
        in_specs=[new(QK), cache, cache,
                  pl.BlockSpec((1, FOX_HEADS, 1, 1, tk), lambda b, j: (layer, 0, b, 0, j)),
                  new(QK), new(FOX_HD),
                  pl.BlockSpec((s_new, LANES), lambda b, j: (b, 0)),
                  pl.BlockSpec((1, SUBLANES, s_new), lambda b, j: (b, 0, 0))],
        out_specs=new(FOX_HD),
        scratch_shapes=[pltpu.VMEM((FOX_HEADS, s_new, LANES), F32),
                        pltpu.VMEM((FOX_HEADS, s_new, LANES), F32),
                        pltpu.VMEM((FOX_HEADS, s_new, FOX_HD), F32)],
        compiler_params=_cparams(("arbitrary", "arbitrary")),
        name="fox_sample_attn",
    )(q, cache_k, cache_v, c_past, k_new, v_new, ccol, crow)


def _mla_sample_kernel(q_ref, wuk_ref, wuv_ref, cc_ref, cr_ref, cn_ref, rn_ref,
                       o_ref, ql_ref, qr_ref, m_ref, l_ref, acc_ref):
    j = pl.program_id(1)
    s_new = q_ref.shape[1]
    dot = functools.partial(jnp.dot, preferred_element_type=F32)

    @pl.when(j == 0)
    def _():
        m_ref[...] = jnp.full_like(m_ref, -jnp.inf)
        l_ref[...] = jnp.zeros_like(l_ref)
        acc_ref[...] = jnp.zeros_like(acc_ref)
        for h in range(MLA_HEADS):
            rs = slice(h * s_new, (h + 1) * s_new)
            ql_ref[rs, :] = dot(q_ref[h, :, :LANES], wuk_ref[h]).astype(BF16)
            qr_ref[rs, :] = q_ref[h, :, LANES:]

    def update(c_keys, s_rope):
        s = _dot_nt(ql_ref[...], c_keys) + s_rope
        m_prev = m_ref[...]
        m_next = jnp.maximum(m_prev, jnp.max(s, axis=1, keepdims=True))
        alpha = jnp.exp2(m_prev - m_next)
        p = jnp.exp2(s - m_next[:, :1])
        l_ref[...] = alpha * l_ref[...] + jnp.sum(p, axis=1, keepdims=True)
        acc_ref[...] = alpha[:, :1] * acc_ref[...] + dot(p.astype(BF16), c_keys)
        m_ref[...] = m_next

    r_t = cr_ref[0, 0].astype(BF16)
    update(cc_ref[0, 0].astype(BF16), dot(qr_ref[...], jnp.concatenate([r_t, jnp.zeros_like(r_t)], axis=0)))

    @pl.when(j == pl.num_programs(1) - 1)
    def _():
        rn = jnp.where(_iota((s_new, LANES), 1) < MLA_ROPE, rn_ref[...], 0.0).astype(BF16)
        update(cn_ref[...].astype(BF16), _dot_nt(qr_ref[...], rn))
        for h in range(MLA_HEADS):
            rs = slice(h * s_new, (h + 1) * s_new)
            o_lat = (acc_ref[rs, :] / l_ref[rs, :1]).astype(BF16)
            o_ref[h] = dot(o_lat, wuv_ref[h]).astype(BF16)


def _mla_sample(q, wuk, wuv, cache_c, cache_r_t, c_new, r_new, layer):
    _, batch, past, _ = cache_c.shape
    s_new = q.shape[1] // batch
    tk = _row_tile(past, SAMPLE_TK)
    rows = MLA_HEADS * s_new
    new = lambda width: pl.BlockSpec((s_new, width), lambda b, j: (b, 0))
    head = lambda width: pl.BlockSpec((MLA_HEADS, s_new, width), lambda b, j: (0, b, 0))
    const = lambda a: _layer_spec(a, layer)
    return pl.pallas_call(
        _mla_sample_kernel,
        out_shape=jax.ShapeDtypeStruct((MLA_HEADS, batch * s_new, MLA_V), BF16),
        grid=(batch, past // tk),
        in_specs=[head(QK), const(wuk), const(wuv),
                  pl.BlockSpec((1, 1, tk, MLA_KV_RANK), lambda b, j: (layer, b, j, 0)),
                  pl.BlockSpec((1, 1, MLA_ROPE, tk), lambda b, j: (layer, b, 0, j)),
                  pl.BlockSpec((None, s_new, MLA_KV_RANK), lambda b, j: (layer, b, 0)), new(LANES)],
        out_specs=head(MLA_V),
        scratch_shapes=[pltpu.VMEM((rows, MLA_KV_RANK), BF16),
                        pltpu.VMEM((rows, LANES), BF16),
                        pltpu.VMEM((rows, LANES), F32),
                        pltpu.VMEM((rows, LANES), F32),
                        pltpu.VMEM((rows, MLA_KV_RANK), F32)],
        compiler_params=_cparams(("arbitrary", "arbitrary")),
        name="mla_sample_attn",
    )(q, wuk, wuv, cache_c, cache_r_t, c_new, r_new)


W_FOX_COLS = 4 * FOX_W + LANES
W_MLA_COLS = MLA_HEADS * QK + MLA_KV_RANK + LANES + MLA_W
W_GLA_COLS = 2 * GLA_HEADS * GLA_DK + GLA_W + LANES + GLA_W
W_IN_BLOCK_COLS = -(-IN_OFFS[-1] // LANES) * LANES
PACK_ROWS = 256


def _pack_kernel(w_ref, fox_ref, mla_ref, gla_ref):
    rows = w_ref.shape[1]
    lane = _iota((rows, LANES), 1)

    def take(src, width):
        start = (src // LANES) * LANES
        shift = src - start
        span = -(-(shift + width) // LANES) * LANES
        win = w_ref[0, :, start:start + span]
        if shift:
            win = pltpu.roll(win, span - shift, axis=1)
        return win[:, :width]

    def narrow(src, width):
        return jnp.where(lane < width, take(src, LANES), 0.0)

    def rope_pair(src):
        p = take(src, LANES)
        quarter = MLA_ROPE // 2
        return jnp.where(lane < MLA_ROPE, p,
                         jnp.where(lane < MLA_ROPE + quarter, pltpu.roll(p, quarter, axis=1),
                                   pltpu.roll(p, LANES - quarter, axis=1)))

    def put(ref, col, val):
        ref[0, :, col:col + val.shape[1]] = val.astype(BF16)

    fq, fk, fv, ff, fg, mq, mc, mr, mg, gq, gk, gv, ggk, gg = IN_OFFS[:-1]
    put(fox_ref, 0, take(fq, FOX_W))
    put(fox_ref, FOX_W, take(fk, FOX_W))
    put(fox_ref, 2 * FOX_W, take(fv, FOX_W))
    put(fox_ref, 3 * FOX_W, take(fg, FOX_W))
    put(fox_ref, 4 * FOX_W, narrow(ff, FOX_HEADS))
    for h in range(MLA_HEADS):
        src = mq + h * (MLA_NOPE + MLA_ROPE)
        put(mla_ref, h * QK, take(src, MLA_NOPE))
        put(mla_ref, h * QK + LANES, rope_pair(src + MLA_NOPE))
    col = MLA_HEADS * QK
    put(mla_ref, col, take(mc, MLA_KV_RANK))
    put(mla_ref, col + MLA_KV_RANK, rope_pair(mr))
    put(mla_ref, col + MLA_KV_RANK + LANES, take(mg, MLA_W))
    dk = GLA_HEADS * GLA_DK
    put(gla_ref, 0, take(gq, dk))
    put(gla_ref, dk, take(gk, dk))
    put(gla_ref, 2 * dk, take(gv, GLA_W))
    put(gla_ref, 2 * dk + GLA_W, narrow(ggk, GLA_GATE_RANK))
    put(gla_ref, 2 * dk + GLA_W + LANES, take(gg, GLA_W))


def _pack_w_in(w_in):
    depth, d, _ = w_in.shape
    tr = _row_tile(d, PACK_ROWS)
    out = lambda cols: pl.BlockSpec((1, tr, cols), lambda l, i: (l, i, 0))
    return pl.pallas_call(
        _pack_kernel,
        out_shape=(jax.ShapeDtypeStruct((depth, d, W_FOX_COLS), BF16),
                   jax.ShapeDtypeStruct((depth, d, W_MLA_COLS), BF16),
                   jax.ShapeDtypeStruct((depth, d, W_GLA_COLS), BF16)),
        grid=(depth, d // tr),
        in_specs=[pl.BlockSpec((1, tr, W_IN_BLOCK_COLS), lambda l, i: (l, i, 0))],
        out_specs=(out(W_FOX_COLS), out(W_MLA_COLS), out(W_GLA_COLS)),
        compiler_params=_cparams(("arbitrary", "arbitrary")),
        name="pack_w_in",
    )(w_in)


def _stacked_params(w_in, b_f, kv_norm, w_up, w_gk_up, b_gk, gla_norm, w_out, ln_g, ln_b):
    depth = w_in.shape[0]
    w_fox, w_mla, w_gla = _pack_w_in(w_in)
    w_up_k = w_up[:, :, :, :MLA_NOPE]
    w_up_v = w_up[:, :, :, MLA_NOPE:]
    return dict(
        w_fox=w_fox,
        b_f=jnp.pad(b_f, ((0, 0), (0, LANES - FOX_HEADS)))[:, None, :],
        w_mla=w_mla,
        kv_norm=kv_norm[:, None, :],
        w_up=jnp.concatenate([w_up_k.reshape(depth, MLA_KV_RANK, MLA_W),
                              w_up_v.reshape(depth, MLA_KV_RANK, MLA_W)], axis=2).astype(BF16),
        w_uk=jnp.transpose(w_up_k, (0, 2, 3, 1)).astype(BF16),
        w_uv=jnp.transpose(w_up_v, (0, 2, 1, 3)).astype(BF16),
        w_gla=w_gla,
        w_gk=jnp.pad(w_gk_up, ((0, 0), (0, LANES - GLA_GATE_RANK), (0, 0))).astype(BF16),
        b_gk=b_gk[:, None, :],
        gla_norm=gla_norm[:, None, :],
        w_out=w_out.astype(BF16),
        ln_g=ln_g[:, None, :],
        ln_b=ln_b[:, None, :],
    )


def _rope_table(pos, rows):
    half = MLA_ROPE // 2
    inv = ROPE_BASE ** (-jnp.arange(half, dtype=F32) / half)
    ang = pos.astype(F32)[:, None] * inv[None, :]
    cos, sin = jnp.cos(ang), jnp.sin(ang)
    tab = jnp.concatenate([cos, cos, -sin, sin], axis=1)
    reps = max(rows // tab.shape[0], 1)
    return jnp.tile(tab, (reps, 1))


def _branches(x2, p, sel, tab, batch, seq, with_kv, layer, prev):
    fox = _fox_proj(x2, p["w_fox"], p["b_f"], sel, batch, seq, layer, prev[:3])
    mla = _mla_proj(x2, p["w_mla"], tab, p["kv_norm"], p["w_up"], with_kv, layer, prev[3:])
    gla = _gla_proj(x2, p["w_gla"], p["w_gk"], p["b_gk"], layer)
    return fox, mla, gla


def kernel(x_prompt, x_sample, cache_fox_k, cache_fox_v, cache_fox_logf, cache_mla_ckv, cache_mla_krope,
           state_gla, w_in, b_fox_f, mla_kv_norm, w_mla_kv_up, w_gla_gk_up, b_gla_gk, gla_norm, w_out,
           ln_g, ln_b):
    depth = w_in.shape[0]
    bp, tp, d = x_prompt.shape
    bs, ts, _ = x_sample.shape
    past = cache_fox_k.shape[2]
    alpha = (2 * depth) ** 0.25
    pairs = GLA_HEADS // 2

    tab_p = _rope_table(jnp.arange(tp), _row_tile(bp * tp, PROJ_ROWS))
    tab_s = _rope_table(past + jnp.arange(ts), _row_tile(bs * ts, PROJ_ROWS))
    sel = _fox_bias_selectors()

    cache_k = jnp.transpose(cache_fox_k, (0, 1, 3, 2, 4))
    cache_v = jnp.transpose(cache_fox_v, (0, 1, 3, 2, 4))
    lf_rows = jnp.transpose(cache_fox_logf.astype(F32), (0, 3, 1, 2))
    c_past = _past_cumsum(lf_rows.reshape(depth * FOX_HEADS * bs, past)).reshape(depth, FOX_HEADS, bs, 1, past)
    cache_r_t = jnp.swapaxes(cache_mla_krope, 2, 3)

    xp = x_prompt.reshape(bp * tp, d)
    xs = x_sample.reshape(bs * ts, d)
    zero_state = jnp.zeros((bp, pairs, LANES, LANES), F32)
    state_s = state_gla.reshape(depth, bs, pairs, LANES, LANES)
    p = _stacked_params(w_in, b_fox_f, mla_kv_norm, w_mla_kv_up, w_gla_gk_up, b_gla_gk, gla_norm, w_out,
                        ln_g, ln_b)
    merge_args = (p["gla_norm"], p["w_out"], p["ln_g"], p["ln_b"], alpha)
    kept_p = _stacked_zeros(depth, bp, tp)
    kept_s = _stacked_zeros(depth, bs, ts)
    gla_p, gla_s = [], []

    for l in range(depth):
        (fk, fv, flf, fq, fkc, fvb, fg, _, _), (mc, mkr, mq, _, mg, mkc, mv), (*gla_in, gg) = \
            _branches(xp, p, sel, tab_p, bp, tp, True, l, kept_p)
        kept_p = (fk, fv, flf, mc, mkr)
        o_f = _prompt_attn(fq, fkc, fvb, bp, tp, chunked=False)
        o_m = _prompt_attn(mq, mkc, mv, bp, tp, chunked=True)
        o_g, s_p = _gla(*gla_in, zero_state, bp, tp)
        gla_p.append(s_p.reshape(bp, GLA_HEADS, GLA_DK, GLA_DV))
        xp = _merge(xp, o_f, fg, o_m, mg, o_g, gg, *merge_args, l)

        (fk, fv, flf, fq, fkc, fvb, fg, fcc, fcr), (mc, mkr, mq, mkr2, mg), (*gla_in, gg) = \
            _branches(xs, p, sel, tab_s, bs, ts, False, l, kept_s)
        kept_s = (fk, fv, flf, mc, mkr)
        crow_s = jnp.swapaxes(fcr.reshape(SUBLANES, bs, ts), 0, 1)
        o_f = _fox_sample(fq, cache_k, cache_v, c_past, fkc, fvb, fcc, crow_s, l)
        o_m = _mla_sample(mq, p["w_uk"], p["w_uv"], cache_mla_ckv, cache_r_t, mc, mkr2, l)
        o_g, s_s = _gla(*gla_in, state_s[l], bs, ts)
        gla_s.append(s_s.reshape(bs, GLA_HEADS, GLA_DK, GLA_DV))
        xs = _merge(xs, o_f, fg, o_m, mg, o_g, gg, *merge_args, l)

    def returned(kept, states, batch, seq):
        fk, fv, flf, mc, mkr = kept
        return (jnp.transpose(fk, (0, 1, 3, 2, 4)), jnp.transpose(fv, (0, 1, 3, 2, 4)),
                flf[:, :, :FOX_HEADS].reshape(depth, batch, seq, FOX_HEADS),
                mc.reshape(depth, batch, seq, MLA_KV_RANK), mkr.reshape(depth, batch, seq, MLA_ROPE),
                jnp.stack(states))

    return ((xp.reshape(bp, tp, d), xs.reshape(bs, ts, d))
            + returned(kept_p, gla_p, bp, tp) + returned(kept_s, gla_s, bs, ts))
```
